```python
import math
import jax
import jax.numpy as jnp
from jax import lax
import numpy as np

D_MODEL = 1024
BATCH = 32
SEQ = 256
DEPTH = 4
DEC_BATCH = 8
DEC_SEQ = 2048
PAST_LEN = 512

GRID_W = 64
N_MIXERS = 3
LAYER_KIND = tuple(l % N_MIXERS for l in range(DEPTH))
LAYER_SLOT = tuple(LAYER_KIND[:l].count(LAYER_KIND[l]) for l in range(DEPTH))
N_DIFF = LAYER_KIND.count(0)
N_RET = LAYER_KIND.count(1)
N_HGRN = LAYER_KIND.count(2)

DA_HEADS = 8
DA_HEAD_DIM = 64
DA_V_DIM = 2 * DA_HEAD_DIM
DA_WIDTH = DA_HEADS * DA_V_DIM
DA_QK = 2 * DA_HEADS * DA_HEAD_DIM
DA_IN = 2 * DA_QK + 2 * DA_WIDTH
ROPE_BASE = 10000.0
Q_BLOCK = 128

RET_HEADS = 8
RET_DK = 128
RET_DV = 256
RET_WIDTH = RET_HEADS * RET_DV
RET_QK = RET_HEADS * RET_DK
RET_IN = 2 * RET_QK + 2 * RET_WIDTH

HG_HEADS = 8
HG_DK = 128
HG_DV = D_MODEL // HG_HEADS
HG_F = HG_HEADS * HG_DK
HG_WIDTH = HG_HEADS * HG_DV
HG_IN = 3 * HG_F + 2 * HG_WIDTH

CHUNK = 64
EPS = 1e-6

kernel_name = 'hybrid_diff_ret_hgrn2_prefix_denoise_step'


def rms_norm(x, gain=None):
    xf = x.astype(jnp.float32)
    y = xf * lax.rsqrt(jnp.mean(xf * xf, axis=-1, keepdims=True) + EPS)
    if gain is not None:
        y = y * gain.astype(jnp.float32)
    return y.astype(x.dtype)


def modulation(cvec, w, b):
    m = jax.nn.silu(cvec) @ w + b
    shift, scale, gate = jnp.split(m, 3, axis=-1)
    return shift[:, None, :], scale[:, None, :], gate[:, None, :]


def axial_rope(x):
    B, T, S, hd = x.shape
    rows = T // GRID_W
    row = jnp.repeat(jnp.arange(rows), GRID_W)
    col = jnp.broadcast_to(jnp.arange(GRID_W), (rows, GRID_W)).reshape(-1)
    n_pair = hd // 4
    inv = ROPE_BASE ** (-jnp.arange(n_pair, dtype=jnp.float32) / n_pair)
    ang = jnp.concatenate([row[:, None] * inv, col[:, None] * inv], axis=-1)
    cos = jnp.cos(ang)[None, :, None, :]
    sin = jnp.sin(ang)[None, :, None, :]
    xf = x.astype(jnp.float32).reshape(B, T, S, hd // 2, 2)
    x1, x2 = xf[..., 0], xf[..., 1]
    out = jnp.stack([x1 * cos - x2 * sin, x1 * sin + x2 * cos], axis=-1)
    return out.reshape(B, T, S, hd).astype(x.dtype)


def to_chunks(x):
    B, T = x.shape[:2]
    return jnp.moveaxis(x.reshape(B, T // CHUNK, CHUNK, *x.shape[2:]), 1, 0)


def from_chunks(x):
    n, B, C = x.shape[:3]
    return jnp.moveaxis(x, 0, 1).reshape(B, n * C, *x.shape[3:])


def diff_softmax_attend(q, k, v, lam):
    B, Tq, S, dh = q.shape
    H = S // 2
    nb = Tq // Q_BLOCK
    qb = jnp.moveaxis(q.reshape(B, nb, Q_BLOCK, S, dh), 1, 0)

    def block(qblk):
        s = jnp.einsum('bqsd,bksd->bsqk', qblk, k).astype(jnp.float32)
        p = jax.nn.softmax(s, axis=-1).reshape(B, H, 2, Q_BLOCK, -1)
        a = p[:, :, 0] - lam * p[:, :, 1]
        return jnp.einsum('bhqk,bkhe->bqhe', a.astype(v.dtype), v)

    o = lax.map(block, qb)
    return jnp.moveaxis(o, 0, 1).reshape(B, Tq, H, v.shape[-1])


def diff_attention(xn, w_in, w_out, lam_vec, sub_g, layer_idx, ctx_k=None, ctx_v=None):
    B, T, _ = xn.shape
    h = xn @ w_in
    q, k, v, g = jnp.split(h, [DA_QK, 2 * DA_QK, 2 * DA_QK + DA_WIDTH], axis=-1)
    q = q.reshape(B, T, 2 * DA_HEADS, DA_HEAD_DIM)
    k = k.reshape(B, T, 2 * DA_HEADS, DA_HEAD_DIM)
    v = v.reshape(B, T, DA_HEADS, DA_V_DIM)
    lam_init = 0.8 - 0.6 * math.exp(-0.3 * layer_idx)
    lf = lam_vec.astype(jnp.float32)
    lam = jnp.exp(jnp.sum(lf[0] * lf[1])) - jnp.exp(jnp.sum(lf[2] * lf[3])) + lam_init
    if ctx_k is None:
        keys, vals = k, v
    else:
        q = axial_rope(q)
        k = axial_rope(k)
        keys = jnp.concatenate([k, ctx_k.astype(k.dtype)], axis=1)
        vals = jnp.concatenate([v, ctx_v.astype(v.dtype)], axis=1)
    o = diff_softmax_attend(q * (DA_HEAD_DIM ** -0.5), keys, vals, lam)
    o = rms_norm(o, sub_g) * (1.0 - lam_init)
    y = (o.reshape(B, T, DA_WIDTH) * jax.nn.silu(g)) @ w_out
    return y, k, v


def retention_scan(q, k, v, log_g, s0):
    idx = jnp.arange(CHUNK, dtype=jnp.float32)
    diff = idx[:, None] - idx[None, :]
    causal = diff >= 0
    decay_mat = jnp.where(causal, jnp.exp(jnp.where(causal, diff, 0.0) * log_g[:, None, None]), 0.0)
    q_dec = jnp.exp((idx + 1.0)[:, None] * log_g[None, :])
    k_dec = jnp.exp((CHUNK - 1.0 - idx)[:, None] * log_g[None, :])
    chunk_dec = jnp.exp(CHUNK * log_g)[None, :, None, None]

    def step(S, inp):
        qc, kc, vc = inp
        sc = jnp.einsum('bthd,bshd->bhts', qc, kc) * decay_mat
        o = (jnp.einsum('bhts,bshe->bthe', sc, vc)
             + jnp.einsum('bthd,bhde->bthe', qc * q_dec[None, :, :, None], S))
        S = chunk_dec * S + jnp.einsum('bshd,bshe->bhde', kc * k_dec[None, :, :, None], vc)
        return S, o

    f32 = jnp.float32
    S, o = lax.scan(step, s0.astype(f32),
                    (to_chunks(q.astype(f32)), to_chunks(k.astype(f32)), to_chunks(v.astype(f32))))
    return from_chunks(o), S


def gla_scan(q, k, v, log_f, s0):
    idx = jnp.arange(CHUNK)
    causal = idx[:, None] >= idx[None, :]
    mid = CHUNK // 2

    def step(S, inp):
        qc, kc, vc, gc = inp
        b = lax.cumsum(gc, axis=1)
        ref = b[:, mid:mid + 1]
        sc = jnp.einsum('bthd,bshd->bhts', qc * jnp.exp(b - ref), kc * jnp.exp(ref - b))
        sc = jnp.where(causal, sc, 0.0)
        o = (jnp.einsum('bhts,bshe->bthe', sc, vc)
             + jnp.einsum('bthd,bhde->bthe', qc * jnp.exp(b), S))
        b_last = b[:, -1:]
        S = (jnp.exp(b_last[:, 0])[..., None] * S
             + jnp.einsum('bshd,bshe->bhde', kc * jnp.exp(b_last - b), vc))
        return S, o

    f32 = jnp.float32
    S, o = lax.scan(step, s0.astype(f32),
                    (to_chunks(q.astype(f32)), to_chunks(k.astype(f32)),
                     to_chunks(v.astype(f32)), to_chunks(log_f.astype(f32))))
    return from_chunks(o), S


def retention(xn, w_in, w_out, decay_param, s0=None):
    B, T, _ = xn.shape
    h = xn @ w_in
    q, k, v, g = jnp.split(h, [RET_QK, 2 * RET_QK, 2 * RET_QK + RET_WIDTH], axis=-1)
    q = q.reshape(B, T, RET_HEADS, RET_DK)
    k = k.reshape(B, T, RET_HEADS, RET_DK) * (RET_DK ** -0.5)
    v = v.reshape(B, T, RET_HEADS, RET_DV)
    log_g = jnp.log1p(-jnp.exp(decay_param.astype(jnp.float32)))
    if s0 is None:
        s0 = jnp.zeros((B, 2, RET_HEADS, RET_DK, RET_DV), jnp.float32)
    of, sf = retention_scan(q, k, v, log_g[0], s0[:, 0])
    ob, sb = retention_scan(q[:, ::-1], k[:, ::-1], v[:, ::-1], log_g[1], s0[:, 1])
    o = rms_norm(of + ob[:, ::-1]).astype(xn.dtype)
    y = (o.reshape(B, T, RET_WIDTH) * jax.nn.silu(g)) @ w_out
    return y, jnp.stack([sf, sb], axis=1)


def hgrn2(xn, w_in, w_out, lb_logits, norm_g, layer_idx, s0=None):
    B, T, _ = xn.shape
    h = xn @ w_in
    q, ff, fb, i, g = jnp.split(h, [HG_F, 2 * HG_F, 3 * HG_F, 3 * HG_F + HG_WIDTH], axis=-1)
    sm = jax.nn.softmax(lb_logits.astype(jnp.float32), axis=1)
    lb = lax.cumsum(sm, axis=1)[:, layer_idx] - sm[:, 0]

    def gates(fl, lbd):
        f = lbd + (1.0 - lbd) * jax.nn.sigmoid(fl.astype(jnp.float32))
        shp = (B, T, HG_HEADS, HG_DK)
        return jnp.log(f).reshape(shp), (1.0 - f).reshape(shp)

    q = jax.nn.silu(q).reshape(B, T, HG_HEADS, HG_DK)
    i = i.reshape(B, T, HG_HEADS, HG_DV)
    gf, kf = gates(ff, lb[0])
    gb, kb = gates(fb, lb[1])
    if s0 is None:
        s0 = jnp.zeros((B, 2, HG_HEADS, HG_DK, HG_DV), jnp.float32)
    of, sf = gla_scan(q, kf, i, gf, s0[:, 0])
    ob, sb = gla_scan(q[:, ::-1], kb[:, ::-1], i[:, ::-1], gb[:, ::-1], s0[:, 1])
    o = rms_norm(of + ob[:, ::-1], norm_g).astype(xn.dtype)
    y = (o.reshape(B, T, HG_WIDTH) * jax.nn.silu(g)) @ w_out
    return y, jnp.stack([sf, sb], axis=1)


def setup_inputs(seed: int = 0) -> dict:
    key = jax.random.key(seed)
    ks = jax.random.split(key, 24)
    f32 = jnp.float32
    D = D_MODEL

    def nrm(k, shape, s):
        return jax.random.normal(k, shape, f32) * s

    ret_base = jnp.asarray(np.log(2.0 ** (-5.0 - np.arange(RET_HEADS))), dtype=f32)
    return {
        'x_prompt': nrm(ks[0], (BATCH, SEQ, D), 1.0),
        'x_sample': nrm(ks[1], (DEC_BATCH, DEC_SEQ, D), 1.0),
        'cache_k': nrm(ks[2], (DEC_BATCH, N_DIFF, PAST_LEN, 2 * DA_HEADS, DA_HEAD_DIM), 1.0),
        'cache_v': nrm(ks[3], (DEC_BATCH, N_DIFF, PAST_LEN, DA_HEADS, DA_V_DIM), 1.0),
        'state_ret': nrm(ks[4], (DEC_BATCH, N_RET, 2, RET_HEADS, RET_DK, RET_DV), 0.5),
        'state_hgrn': nrm(ks[5], (DEC_BATCH, N_HGRN, 2, HG_HEADS, HG_DK, HG_DV), 0.5),
        'c': nrm(ks[6], (DEC_BATCH, D), 1.0),
        'c_ctx': nrm(ks[7], (D,), 1.0),
        'w_mod': nrm(ks[8], (DEPTH, D, 3 * D), 0.5 * D ** -0.5),
        'b_mod': nrm(ks[9], (DEPTH, 3 * D), 0.02),
        'g_pre': 1.0 + nrm(ks[10], (DEPTH, D), 0.05),
        'g_post': 1.0 + nrm(ks[11], (DEPTH, D), 0.05),
        'da_w_in': nrm(ks[12], (N_DIFF, D, DA_IN), D ** -0.5),
        'da_w_out': nrm(ks[13], (N_DIFF, DA_WIDTH, D), DA_WIDTH ** -0.5),
        'da_lambda': nrm(ks[14], (N_DIFF, 4, DA_HEAD_DIM), 0.1),
        'da_subln': 1.0 + nrm(ks[15], (N_DIFF, DA_V_DIM), 0.05),
        'ret_w_in': nrm(ks[16], (N_RET, D, RET_IN), D ** -0.5),
        'ret_w_out': nrm(ks[17], (N_RET, RET_WIDTH, D), RET_WIDTH ** -0.5),
        'ret_decay': ret_base + nrm(ks[18], (N_RET, 2, RET_HEADS), 0.05),
        'hg_w_in': nrm(ks[19], (N_HGRN, D, HG_IN), D ** -0.5),
        'hg_w_out': nrm(ks[20], (N_HGRN, HG_WIDTH, D), HG_WIDTH ** -0.5),
        'hg_lb': nrm(ks[21], (2, DEPTH, HG_F), 0.1),
        'hg_norm': 1.0 + nrm(ks[22], (N_HGRN, HG_DV), 0.05),
    }


def reference(x_prompt, x_sample, cache_k, cache_v, state_ret, state_hgrn, c, c_ctx,
              w_mod, b_mod, g_pre, g_post,
              da_w_in, da_w_out, da_lambda, da_subln,
              ret_w_in, ret_w_out, ret_decay,
              hg_w_in, hg_w_out, hg_lb, hg_norm):
    xp, xs = x_prompt, x_sample
    new_k, new_v, new_r, new_h = [], [], [], []
    for l in range(DEPTH):
        kind = LAYER_KIND[l]
        j = LAYER_SLOT[l]
        sh_p, sc_p, ga_p = modulation(c_ctx[None, :], w_mod[l], b_mod[l])
        sh_s, sc_s, ga_s = modulation(c, w_mod[l], b_mod[l])
        hp = rms_norm(xp, g_pre[l]) * (1.0 + sc_p) + sh_p
        hs = rms_norm(xs, g_pre[l]) * (1.0 + sc_s) + sh_s
        if kind == 0:
            yp, kp, vp = diff_attention(hp, da_w_in[j], da_w_out[j], da_lambda[j], da_subln[j], l)
            ys, _, _ = diff_attention(hs, da_w_in[j], da_w_out[j], da_lambda[j], da_subln[j], l,
                                      cache_k[:, j], cache_v[:, j])
            new_k.append(kp)
            new_v.append(vp)
        elif kind == 1:
            yp, sp = retention(hp, ret_w_in[j], ret_w_out[j], ret_decay[j])
            ys, _ = retention(hs, ret_w_in[j], ret_w_out[j], ret_decay[j], state_ret[:, j])
            new_r.append(sp)
        else:
            yp, sp = hgrn2(hp, hg_w_in[j], hg_w_out[j], hg_lb, hg_norm[j], l)
            ys, _ = hgrn2(hs, hg_w_in[j], hg_w_out[j], hg_lb, hg_norm[j], l, state_hgrn[:, j])
            new_h.append(sp)
        xp = xp + ga_p * rms_norm(yp, g_post[l])
        xs = xs + ga_s * rms_norm(ys, g_post[l])
    new_cache_k = jnp.stack(new_k, axis=1)
    new_cache_v = jnp.stack(new_v, axis=1)
    new_state_ret = jnp.stack(new_r, axis=1)
    new_state_hgrn = jnp.stack(new_h, axis=1)
    return (xp, xs, new_cache_k, new_cache_v, new_state_ret, new_state_hgrn)
```

```python
import functools
import math

import jax
import jax.numpy as jnp
from jax import lax
from jax.experimental import pallas as pl
from jax.experimental.pallas import tpu as pltpu

F32 = jnp.float32
BF16 = jnp.bfloat16

EPS = 1e-6
LANES = 128
GRID_W = 64
ROPE_BASE = 10000.0
N_HEADS = 8
DA_HEAD_DIM = 64
DA_V_DIM = 128
RET_DK = 128
RET_DV = 256
HG_DK = 128
HG_DV = 128
RET_CHUNK = 128
GLA_CHUNK = 64
VMEM_LIMIT = 56 * 1024 * 1024


def _cparams(*sem):
    return pltpu.CompilerParams(dimension_semantics=sem, vmem_limit_bytes=VMEM_LIMIT)


def _dot(a, b):
    return jnp.dot(a, b, preferred_element_type=F32)


def _dot_nt(a, b):
    return lax.dot_general(a, b, (((1,), (1,)), ((), ())), preferred_element_type=F32)


def _dot_tn(a, b):
    return lax.dot_general(a, b, (((0,), (0,)), ((), ())), preferred_element_type=F32)


def _silu(x):
    return x * jax.nn.sigmoid(x)


def _rms(x):
    return x * lax.rsqrt(jnp.mean(x * x, axis=-1, keepdims=True) + EPS)


def _mod_kernel(c_ref, w_ref, b_ref, o_ref):
    s = _silu(c_ref[...]).astype(BF16)
    o_ref[0] = _dot(s, w_ref[0].astype(BF16)) + b_ref[0]


def _modulation_all(cvec, w_mod, b_mod):
    depth, d, n = w_mod.shape
    r = cvec.shape[0]
    tn = 1024
    return pl.pallas_call(
        _mod_kernel,
        grid=(depth, n // tn),
        in_specs=[
            pl.BlockSpec((r, d), lambda l, j: (0, 0)),
            pl.BlockSpec((1, d, tn), lambda l, j: (l, 0, j)),
            pl.BlockSpec((1, 1, tn), lambda l, j: (l, 0, j)),
        ],
        out_specs=pl.BlockSpec((1, r, tn), lambda l, j: (l, 0, j)),
        out_shape=jax.ShapeDtypeStruct((depth, r, n), F32),
        compiler_params=_cparams("parallel", "parallel"),
        name="modulation",
    )(cvec, w_mod, b_mod.reshape(depth, 1, n))


PROJ_NC = 512


def _norm_mod(x_ref, gpre_ref, mod_ref):
    d = x_ref.shape[-1]
    xn = _rms(x_ref[0]) * gpre_ref[...]
    shift = mod_ref[0, :, 0:d]
    scale = mod_ref[0, :, d:2 * d]
    return (xn * (1.0 + scale) + shift).astype(BF16)


def _store_heads(ref, acc, col0, dh, dtype):
    for j in range(acc.shape[1] // dh):
        ref[0, col0 // dh + j] = acc[:, j * dh:(j + 1) * dh].astype(dtype)


def _da_proj_kernel(*refs, rope, emit_cache):
    x_ref, gpre_ref, mod_ref, w_ref = refs[:4]
    i = 4
    if rope:
        cos_ref, sinp_ref, sinn_ref = refs[i:i + 3]
        i += 3
    q_ref, k_ref, v_ref, g_ref = refs[i:i + 4]
    i += 4
    if emit_cache:
        kc_ref, vc_ref = refs[i:i + 2]
    d = x_ref.shape[-1]
    h = _norm_mod(x_ref, gpre_ref, mod_ref)

    def rot(a):
        out = []
        for j in range(a.shape[1] // LANES):
            t = a[:, j * LANES:(j + 1) * LANES]
            out.append(t * cos_ref[...] + pltpu.roll(t, 1, 1) * sinp_ref[...]
                       + pltpu.roll(t, LANES - 1, 1) * sinn_ref[...])
        return jnp.concatenate(out, axis=1)

    for sec in range(4):
        for c0 in range(0, d, PROJ_NC):
            acc = _dot(h, w_ref[:, sec * d + c0: sec * d + c0 + PROJ_NC])
            if sec == 0:
                if rope:
                    acc = rot(acc)
                _store_heads(q_ref, acc * (DA_HEAD_DIM ** -0.5), c0, LANES, BF16)
            elif sec == 1:
                if emit_cache:
                    kc_ref[0, :, c0:c0 + PROJ_NC] = acc
                if rope:
                    acc = rot(acc)
                _store_heads(k_ref, acc, c0, LANES, BF16)
            elif sec == 2:
                if emit_cache:
                    vc_ref[0, :, c0:c0 + PROJ_NC] = acc
                _store_heads(v_ref, acc, c0, DA_V_DIM, BF16)
            else:
                g_ref[0, :, c0:c0 + PROJ_NC] = acc.astype(BF16)


def _ret_proj_kernel(x_ref, gpre_ref, mod_ref, w_ref, q_ref, k_ref, v_ref, g_ref):
    d = x_ref.shape[-1]
    h = _norm_mod(x_ref, gpre_ref, mod_ref)
    qk = N_HEADS * RET_DK
    vw = N_HEADS * RET_DV
    for c0 in range(0, qk, PROJ_NC):
        _store_heads(q_ref, _dot(h, w_ref[:, c0:c0 + PROJ_NC]), c0, RET_DK, BF16)
    for c0 in range(0, qk, PROJ_NC):
        acc = _dot(h, w_ref[:, qk + c0: qk + c0 + PROJ_NC]) * (RET_DK ** -0.5)
        _store_heads(k_ref, acc, c0, RET_DK, BF16)
    for c0 in range(0, vw, PROJ_NC):
        _store_heads(v_ref, _dot(h, w_ref[:, 2 * qk + c0: 2 * qk + c0 + PROJ_NC]), c0, RET_DV, BF16)
    for c0 in range(0, vw, PROJ_NC):
        g_ref[0, :, c0:c0 + PROJ_NC] = _dot(
            h, w_ref[:, 2 * qk + vw + c0: 2 * qk + vw + c0 + PROJ_NC]).astype(BF16)


def _hg_proj_kernel(x_ref, gpre_ref, mod_ref, w_ref, lb_ref,
                    q_ref, kf_ref, kb_ref, gf_ref, gb_ref, v_ref, g_ref, *, layer_idx, depth):
    d = x_ref.shape[-1]
    h = _norm_mod(x_ref, gpre_ref, mod_ref)
    f_w = N_HEADS * HG_DK

    def lower_bound(direction):
        lg = lb_ref[direction * depth:(direction + 1) * depth, :]
        e = jnp.exp(lg - jnp.max(lg, axis=0, keepdims=True))
        sm = e / jnp.sum(e, axis=0, keepdims=True)
        cum = sm[0:1]
        for i in range(1, layer_idx + 1):
            cum = cum + sm[i:i + 1]
        return cum - sm[0:1]

    lbs = (lower_bound(0), lower_bound(1))
    for c0 in range(0, f_w, PROJ_NC):
        _store_heads(q_ref, _silu(_dot(h, w_ref[:, c0:c0 + PROJ_NC])), c0, HG_DK, BF16)
    for direction, (k_out, g_out) in enumerate(((kf_ref, gf_ref), (kb_ref, gb_ref))):
        base = (1 + direction) * f_w
        for c0 in range(0, f_w, PROJ_NC):
            acc = _dot(h, w_ref[:, base + c0: base + c0 + PROJ_NC])
            lbd = lbs[direction][:, c0:c0 + PROJ_NC]
            f = lbd + (1.0 - lbd) * jax.nn.sigmoid(acc)
            _store_heads(g_out, jnp.log(f), c0, HG_DK, F32)
            _store_heads(k_out, 1.0 - f, c0, HG_DK, BF16)
    for c0 in range(0, d, PROJ_NC):
        _store_heads(v_ref, _dot(h, w_ref[:, 3 * f_w + c0: 3 * f_w + c0 + PROJ_NC]), c0, HG_DV, BF16)
    for c0 in range(0, d, PROJ_NC):
        g_ref[0, :, c0:c0 + PROJ_NC] = _dot(
            h, w_ref[:, 3 * f_w + d + c0: 3 * f_w + d + c0 + PROJ_NC]).astype(BF16)


def _proj_call(kernel, x, mod, gpre, w, extra_in, extra_specs, outs, name, tm):
    b, t, d = x.shape
    n = w.shape[1]
    bm = mod.shape[0]
    mod_idx = (lambda bi, i: (bi, 0, 0)) if bm > 1 else (lambda bi, i: (0, 0, 0))
    in_specs = [
        pl.BlockSpec((1, tm, d), lambda bi, i: (bi, i, 0)),
        pl.BlockSpec((1, d), lambda bi, i: (0, 0)),
        pl.BlockSpec((1, 1, mod.shape[2]), mod_idx),
        pl.BlockSpec((d, n), lambda bi, i: (0, 0)),
    ] + extra_specs
    out_specs, out_shape = [], []
    for kind, width, dtype in outs:
        if kind == "heads":
            out_specs.append(pl.BlockSpec((1, N_HEADS, tm, width), lambda bi, i: (bi, 0, i, 0)))
            out_shape.append(jax.ShapeDtypeStruct((b, N_HEADS, t, width), dtype))
        else:
            out_specs.append(pl.BlockSpec((1, tm, width), lambda bi, i: (bi, i, 0)))
            out_shape.append(jax.ShapeDtypeStruct((b, t, width), dtype))
    return pl.pallas_call(
        kernel,
        grid=(b, t // tm),
        in_specs=in_specs,
        out_specs=out_specs,
        out_shape=out_shape,
        compiler_params=_cparams("parallel", "parallel"),
        name=name,
    )(x, gpre, mod, w, *extra_in)


def _attn_kernel(lam_ref, subg_ref, q_ref, k_ref, v_ref, o_ref, s_ref, *, tq, kc, nkc, lam_init):
    q = q_ref[0, 0]
    lane = lax.broadcasted_iota(jnp.int32, (tq, LANES), 1)
    zero = jnp.zeros_like(q)
    qs = jnp.concatenate([jnp.where(lane < DA_HEAD_DIM, q, zero),
                          jnp.where(lane >= DA_HEAD_DIM, q, zero)], axis=0)
    m = jnp.full((2 * tq, LANES), -jnp.inf, F32)
    for c in range(nkc):
        s = _dot_nt(qs, k_ref[0, 0, c * kc:(c + 1) * kc, :])
        s_ref[c] = s
        for j in range(kc // LANES):
            m = jnp.maximum(m, s[:, j * LANES:(j + 1) * LANES])
    m_row = jnp.max(m, axis=-1, keepdims=True)
    l = jnp.zeros((2 * tq, LANES), F32)
    acc = jnp.zeros((2 * tq, DA_V_DIM), F32)
    for c in range(nkc):
        p = jnp.exp(s_ref[c] - m_row)
        for j in range(kc // LANES):
            l = l + p[:, j * LANES:(j + 1) * LANES]
        acc = acc + _dot(p.astype(BF16), v_ref[0, 0, c * kc:(c + 1) * kc, :])
    on = acc / jnp.sum(l, axis=-1, keepdims=True)
    lf = lam_ref[...]
    lam = (jnp.exp(jnp.sum(lf[0:1] * lf[1:2], axis=-1, keepdims=True))
           - jnp.exp(jnp.sum(lf[2:3] * lf[3:4], axis=-1, keepdims=True)) + lam_init)
    o = on[:tq] - lam * on[tq:]
    o = _rms(o) * subg_ref[...] * (1.0 - lam_init)
    o_ref[0, 0] = o.astype(BF16)


def _attention(q, k, v, lam_vec, sub_g, layer_idx, tq):
    b, nh, t, _ = q.shape
    tk = k.shape[2]
    kc = next(w for w in (512, 384, 256, 128) if tk % w == 0)
    nkc = tk // kc
    lam_init = 0.8 - 0.6 * math.exp(-0.3 * layer_idx)
    kern = functools.partial(_attn_kernel, tq=tq, kc=kc, nkc=nkc, lam_init=lam_init)
    return pl.pallas_call(
        kern,
        grid=(b, nh, t // tq),
        in_specs=[
            pl.BlockSpec(lam_vec.shape, lambda bi, h, i: (0, 0)),
            pl.BlockSpec((1, DA_V_DIM), lambda bi, h, i: (0, 0)),
            pl.BlockSpec((1, 1, tq, LANES), lambda bi, h, i: (bi, h, i, 0)),
            pl.BlockSpec((1, 1, tk, LANES), lambda bi, h, i: (bi, h, 0, 0)),
            pl.BlockSpec((1, 1, tk, DA_V_DIM), lambda bi, h, i: (bi, h, 0, 0)),
        ],
        out_specs=pl.BlockSpec((1, 1, tq, DA_V_DIM), lambda bi, h, i: (bi, h, i, 0)),
        out_shape=jax.ShapeDtypeStruct((b, nh, t, DA_V_DIM), BF16),
        scratch_shapes=[pltpu.VMEM((nkc, 2 * tq, kc), F32)],
        compiler_params=_cparams("parallel", "parallel", "arbitrary"),
        name="diff_attention",
    )(lam_vec, sub_g.reshape(1, DA_V_DIM), q, k, v)


def _ret_kernel(*refs, chunk, nc, has_s0, emit_state):
    dec_ref, q_ref, k_ref, v_ref = refs[:4]
    i = 4
    if has_s0:
        s0_ref = refs[i]
        i += 1
    o_ref = refs[i]
    i += 1
    if emit_state:
        st_ref = refs[i]
        i += 1
    sf_ref, sb_ref = refs[i:i + 2]
    c_ = chunk
    lg = jnp.log1p(-jnp.exp(dec_ref[0]))
    lgf, lgb = lg[0:1, :RET_DK], lg[1:2, :RET_DK]
    row = lax.broadcasted_iota(jnp.int32, (c_, RET_DK), 0).astype(F32)
    k_dec_f = jnp.exp((c_ - 1.0 - row) * lgf)
    k_dec_b = jnp.exp(row * lgb)
    q_dec_f = jnp.exp((row + 1.0) * lgf)
    q_dec_b = jnp.exp((c_ - row) * lgb)
    chunk_dec_f = jnp.exp(c_ * lg[0:1])
    chunk_dec_b = jnp.exp(c_ * lg[1:2])
    diff = row - lax.broadcasted_iota(jnp.int32, (c_, c_), 1).astype(F32)
    dmat = (jnp.where(diff >= 0, jnp.exp(jnp.where(diff >= 0, diff, 0.0) * lgf), 0.0)
            + jnp.where(diff <= 0, jnp.exp(jnp.where(diff <= 0, -diff, 0.0) * lgb), 0.0))

    def rows(c):
        return pl.ds(pl.multiple_of(c * c_, c_), c_)

    def chunk_kv(c, carry):
        kc = k_ref[0, 0, rows(c), :].astype(F32)
        vc = v_ref[0, 0, rows(c), :]
        sf_ref[c] = _dot_tn((kc * k_dec_f).astype(BF16), vc)
        sb_ref[c] = _dot_tn((kc * k_dec_b).astype(BF16), vc)
        return carry

    lax.fori_loop(0, nc, chunk_kv, 0)

    if has_s0:
        s_f, s_b = s0_ref[0, 0, 0], s0_ref[0, 1, 0]
    else:
        s_f = s_b = jnp.zeros((RET_DK, RET_DV), F32)

    def scan_f(c, s):
        u = sf_ref[c]
        sf_ref[c] = s
        return chunk_dec_f * s + u

    def scan_b(j, s):
        c = nc - 1 - j
        u = sb_ref[c]
        sb_ref[c] = s
        return chunk_dec_b * s + u

    s_f = lax.fori_loop(0, nc, scan_f, s_f)
    s_b = lax.fori_loop(0, nc, scan_b, s_b)
    if emit_state:
        st_ref[0, 0, 0] = s_f
        st_ref[0, 1, 0] = s_b

    def chunk_out(c, carry):
        qc = q_ref[0, 0, rows(c), :]
        kc = k_ref[0, 0, rows(c), :]
        vc = v_ref[0, 0, rows(c), :]
        sc = (_dot_nt(qc, kc) * dmat).astype(BF16)
        qf = qc.astype(F32)
        qcat = jnp.concatenate([(qf * q_dec_f).astype(BF16), (qf * q_dec_b).astype(BF16)], axis=1)
        scat = jnp.concatenate([sf_ref[c], sb_ref[c]], axis=0).astype(BF16)
        o = _dot(sc, vc) + _dot(qcat, scat)
        o_ref[0, 0, rows(c), :] = _rms(o).astype(BF16)
        return carry

    lax.fori_loop(0, nc, chunk_out, 0)


def _retention(q, k, v, dec, s0, emit_state):
    b, nh, t, _ = q.shape
    c_ = RET_CHUNK
    nc = t // c_
    has_s0 = s0 is not None
    kern = functools.partial(_ret_kernel, chunk=c_, nc=nc, has_s0=has_s0, emit_state=emit_state)
    in_specs = [
        pl.BlockSpec((1, 2, RET_DV), lambda bi, h: (h, 0, 0)),
        pl.BlockSpec((1, 1, t, RET_DK), lambda bi, h: (bi, h, 0, 0)),
        pl.BlockSpec((1, 1, t, RET_DK), lambda bi, h: (bi, h, 0, 0)),
        pl.BlockSpec((1, 1, t, RET_DV), lambda bi, h: (bi, h, 0, 0)),
    ]
    args = [dec, q, k, v]
    st_spec = pl.BlockSpec((1, 2, 1, RET_DK, RET_DV), lambda bi, h: (bi, 0, h, 0, 0))
    if has_s0:
        in_specs.append(st_spec)
        args.append(s0)
    out_specs = [pl.BlockSpec((1, 1, t, RET_DV), lambda bi, h: (bi, h, 0, 0))]
    out_shape = [jax.ShapeDtypeStruct((b, nh, t, RET_DV), BF16)]
    if emit_state:
        out_specs.append(st_spec)
        out_shape.append(jax.ShapeDtypeStruct((b, 2, nh, RET_DK, RET_DV), F32))
    return pl.pallas_call(
        kern,
        grid=(b, nh),
        in_specs=in_specs,
        out_specs=out_specs,
        out_shape=out_shape,
        scratch_shapes=[pltpu.VMEM((nc, RET_DK, RET_DV), F32)] * 2,
        compiler_params=_cparams("parallel", "parallel"),
        name="retention",
    )(*args)


def _gla_kernel(*refs, chunk, nc, has_s0, emit_state):
    ng_ref, q_ref, kf_ref, kb_ref, gf_ref, gb_ref, v_ref = refs[:7]
    i = 7
    if has_s0:
        s0_ref = refs[i]
        i += 1
    o_ref = refs[i]
    i += 1
    if emit_state:
        st_ref = refs[i]
        i += 1
    sf_ref, sb_ref, bf_ref, bb_ref = refs[i:i + 4]
    c_ = chunk
    mid = c_ // 2
    r = lax.broadcasted_iota(jnp.int32, (c_, c_), 0)
    cc = lax.broadcasted_iota(jnp.int32, (c_, c_), 1)
    causal = r >= cc
    anti = r <= cc
    tri_f = causal.astype(BF16)
    tri_b = anti.astype(BF16)

    def rows(c):
        return pl.ds(pl.multiple_of(c * c_, c_), c_)

    def cumsum(tri, g):
        hi = g.astype(BF16)
        lo = (g - hi.astype(F32)).astype(BF16)
        return _dot(tri, hi) + _dot(tri, lo)

    def chunk_kv(c, carry):
        v_t = v_ref[0, 0, rows(c), :]
        bf = cumsum(tri_f, gf_ref[0, 0, rows(c), :])
        bb = cumsum(tri_b, gb_ref[0, 0, rows(c), :])
        bf_ref[rows(c), :] = bf
        bb_ref[rows(c), :] = bb
        kdf = (kf_ref[0, 0, rows(c), :].astype(F32) * jnp.exp(bf[c_ - 1:c_] - bf)).astype(BF16)
        kdb = (kb_ref[0, 0, rows(c), :].astype(F32) * jnp.exp(bb[0:1] - bb)).astype(BF16)
        sf_ref[c] = _dot_tn(v_t, kdf)
        sb_ref[c] = _dot_tn(v_t, kdb)
        return carry

    lax.fori_loop(0, nc, chunk_kv, 0)

    if has_s0:
        s_f, s_b = s0_ref[0, 0, 0].T, s0_ref[0, 1, 0].T
    else:
        s_f = s_b = jnp.zeros((HG_DV, HG_DK), F32)

    def scan_f(c, s):
        u = sf_ref[c]
        sf_ref[c] = s
        last = pl.multiple_of(c * c_, c_) + (c_ - 1)
        return jnp.exp(bf_ref[pl.ds(last, 1), :]) * s + u

    def scan_b(j, s):
        c = nc - 1 - j
        u = sb_ref[c]
        sb_ref[c] = s
        first = pl.multiple_of(c * c_, c_)
        return jnp.exp(bb_ref[pl.ds(first, 1), :]) * s + u

    s_f = lax.fori_loop(0, nc, scan_f, s_f)
    s_b = lax.fori_loop(0, nc, scan_b, s_b)
    if emit_state:
        st_ref[0, 0, 0] = s_f.T
        st_ref[0, 1, 0] = s_b.T

    def chunk_out(c, carry):
        q = q_ref[0, 0, rows(c), :].astype(F32)
        kf = kf_ref[0, 0, rows(c), :].astype(F32)
        kb = kb_ref[0, 0, rows(c), :].astype(F32)
        bf = bf_ref[rows(c), :]
        bb = bb_ref[rows(c), :]
        ref_f = bf[mid:mid + 1]
        ref_b = bb[c_ - 1 - mid:c_ - mid]
        sc_f = _dot_nt((q * jnp.exp(bf - ref_f)).astype(BF16), (kf * jnp.exp(ref_f - bf)).astype(BF16))
        sc_b = _dot_nt((q * jnp.exp(bb - ref_b)).astype(BF16), (kb * jnp.exp(ref_b - bb)).astype(BF16))
        sc = jnp.where(causal, sc_f, 0.0) + jnp.where(anti, sc_b, 0.0)
        qcat = jnp.concatenate([(q * jnp.exp(bf)).astype(BF16), (q * jnp.exp(bb)).astype(BF16)], axis=1)
        scat = jnp.concatenate([sf_ref[c], sb_ref[c]], axis=1).astype(BF16)
        o = _dot(sc.astype(BF16), v_ref[0, 0, rows(c), :]) + _dot_nt(qcat, scat)
        o_ref[0, 0, rows(c), :] = (_rms(o) * ng_ref[...]).astype(BF16)
        return carry

    lax.fori_loop(0, nc, chunk_out, 0)


def _gla(q, kf, kb, gf, gb, v, norm_g, s0, emit_state):
    b, nh, t, _ = q.shape
    c_ = GLA_CHUNK
    nc = t // c_
    has_s0 = s0 is not None
    kern = functools.partial(_gla_kernel, chunk=c_, nc=nc, has_s0=has_s0, emit_state=emit_state)
    seq = pl.BlockSpec((1, 1, t, HG_DK), lambda bi, h: (bi, h, 0, 0))
    in_specs = [pl.BlockSpec((1, HG_DV), lambda bi, h: (0, 0))] + [seq] * 6
    args = [norm_g.reshape(1, HG_DV), q, kf, kb, gf, gb, v]
    st_spec = pl.BlockSpec((1, 2, 1, HG_DK, HG_DV), lambda bi, h: (bi, 0, h, 0, 0))
    if has_s0:
        in_specs.append(st_spec)
        args.append(s0)
    out_specs = [pl.BlockSpec((1, 1, t, HG_DV), lambda bi, h: (bi, h, 0, 0))]
    out_shape = [jax.ShapeDtypeStruct((b, nh, t, HG_DV), BF16)]
    if emit_state:
        out_specs.append(st_spec)
        out_shape.append(jax.ShapeDtypeStruct((b, 2, nh, HG_DK, HG_DV), F32))
    return pl.pallas_call(
        kern,
        grid=(b, nh),
        in_specs=in_specs,
        out_specs=out_specs,
        out_shape=out_shape,
        scratch_shapes=[pltpu.VMEM((nc, HG_DV, HG_DK), F32)] * 2 + [pltpu.VMEM((t, HG_DK), F32)] * 2,
        compiler_params=_cparams("parallel", "parallel"),
        name="gla",
    )(*args)


def _out_kernel(o_ref, g_ref, x_ref, w_ref, mod_ref, gpost_ref, y_ref):
    d = x_ref.shape[-1]
    o = jnp.concatenate([o_ref[0, h] for h in range(o_ref.shape[1])], axis=-1).astype(F32)
    z = (o * _silu(g_ref[0].astype(F32))).astype(BF16)
    y = _rms(_dot(z, w_ref[...])) * gpost_ref[...]
    y_ref[0] = x_ref[0] + mod_ref[0, :, 2 * d:3 * d] * y


def _out_call(o, g, x, w, mod, gpost, tm):
    b, t, d = x.shape
    nh, dv = o.shape[1], o.shape[3]
    wd = nh * dv
    bm = mod.shape[0]
    mod_idx = (lambda bi, i: (bi, 0, 0)) if bm > 1 else (lambda bi, i: (0, 0, 0))
    return pl.pallas_call(
        _out_kernel,
        grid=(b, t // tm),
        in_specs=[
            pl.BlockSpec((1, nh, tm, dv), lambda bi, i: (bi, 0, i, 0)),
            pl.BlockSpec((1, tm, wd), lambda bi, i: (bi, i, 0)),
            pl.BlockSpec((1, tm, d), lambda bi, i: (bi, i, 0)),
            pl.BlockSpec((wd, d), lambda bi, i: (0, 0)),
            pl.BlockSpec((1, 1, mod.shape[2]), mod_idx),
            pl.BlockSpec((1, d), lambda bi, i: (0, 0)),
        ],
        out_specs=pl.BlockSpec((1, tm, d), lambda bi, i: (bi, i, 0)),
        out_shape=jax.ShapeDtypeStruct((b, t, d), F32),
        compiler_params=_cparams("parallel", "parallel"),
        name="out_proj",
    )(o, g, x, w, mod, gpost)


def _rope_tables(t):
    rows = t // GRID_W
    row = jnp.repeat(jnp.arange(rows), GRID_W)
    col = jnp.broadcast_to(jnp.arange(GRID_W), (rows, GRID_W)).reshape(-1)
    n_pair = DA_HEAD_DIM // 4
    inv = ROPE_BASE ** (-jnp.arange(n_pair, dtype=F32) / n_pair)
    ang = jnp.concatenate([row[:, None] * inv, col[:, None] * inv], axis=-1)
    cos = jnp.repeat(jnp.cos(ang), 2, axis=-1)
    sin = jnp.repeat(jnp.sin(ang), 2, axis=-1)
    odd = (jnp.arange(DA_HEAD_DIM) % 2) == 1
    sin_prev = jnp.where(odd, sin, 0.0)
    sin_next = jnp.where(odd, 0.0, -sin)
    tile = lambda a: jnp.tile(a, (1, LANES // DA_HEAD_DIM))
    return tile(cos), tile(sin_prev), tile(sin_next)


def kernel(x_prompt, x_sample, cache_k, cache_v, state_ret, state_hgrn, c, c_ctx,
           w_mod, b_mod, g_pre, g_post,
           da_w_in, da_w_out, da_lambda, da_subln,
           ret_w_in, ret_w_out, ret_decay,
           hg_w_in, hg_w_out, hg_lb, hg_norm):
    depth, d, _ = w_mod.shape
    db = x_sample.shape[0]
    ts = x_sample.shape[1]
    tp = x_prompt.shape[1]
    n_mix = 3
    kinds = [l % n_mix for l in range(depth)]
    slots = [kinds[:l].count(kinds[l]) for l in range(depth)]

    cvec = jnp.zeros((16, d), F32).at[:db].set(c).at[db].set(c_ctx)
    mods = _modulation_all(cvec, w_mod, b_mod)

    rope = _rope_tables(ts)
    tab_spec = pl.BlockSpec((256, LANES), lambda bi, i: (i, 0))
    tm = 256

    xp, xs = x_prompt, x_sample
    new_k, new_v, new_r, new_h = [], [], [], []
    for l in range(depth):
        kind, j = kinds[l], slots[l]
        mod_p = mods[l, db:db + 1].reshape(1, 1, 3 * d)
        mod_s = mods[l, :db].reshape(db, 1, 3 * d)
        gpre = g_pre[l].reshape(1, d)
        gpost = g_post[l].reshape(1, d)
        if kind == 0:
            w_in = da_w_in[j].astype(BF16)
            w_out = da_w_out[j].astype(BF16)
            heads = [("heads", LANES, BF16)] * 3 + [("plain", d, BF16)]
            qp, kp, vp, gp, kcache, vcache = _proj_call(
                functools.partial(_da_proj_kernel, rope=False, emit_cache=True),
                xp, mod_p, gpre, w_in, [], [], heads + [("plain", d, F32)] * 2, "da_proj_prompt", tm)
            qs, ks, vs, gs = _proj_call(
                functools.partial(_da_proj_kernel, rope=True, emit_cache=False),
                xs, mod_s, gpre, w_in, list(rope), [tab_spec] * 3, heads, "da_proj_sample", tm)
            new_k.append(kcache)
            new_v.append(vcache)
            ck = cache_k[:, j].reshape(db, -1, N_HEADS, LANES).transpose(0, 2, 1, 3).astype(BF16)
            cv = cache_v[:, j].transpose(0, 2, 1, 3).astype(BF16)
            ks = jnp.concatenate([ks, ck], axis=2)
            vs = jnp.concatenate([vs, cv], axis=2)
            op = _attention(qp, kp, vp, da_lambda[j], da_subln[j], l, tq=256)
            os_ = _attention(qs, ks, vs, da_lambda[j], da_subln[j], l, tq=256)
        elif kind == 1:
            w_in = ret_w_in[j].astype(BF16)
            w_out = ret_w_out[j].astype(BF16)
            outs = [("heads", RET_DK, BF16)] * 2 + [("heads", RET_DV, BF16), ("plain", N_HEADS * RET_DV, BF16)]
            qp, kp, vp, gp = _proj_call(_ret_proj_kernel, xp, mod_p, gpre, w_in, [], [], outs,
                                        "ret_proj_prompt", tm)
            qs, ks, vs, gs = _proj_call(_ret_proj_kernel, xs, mod_s, gpre, w_in, [], [], outs,
                                        "ret_proj_sample", tm)
            dec = jnp.broadcast_to(ret_decay[j].T[:, :, None], (N_HEADS, 2, RET_DV))
            op, st = _retention(qp, kp, vp, dec, None, True)
            (os_,) = _retention(qs, ks, vs, dec, state_ret[:, j], False)
            new_r.append(st)
        else:
            w_in = hg_w_in[j].astype(BF16)
            w_out = hg_w_out[j].astype(BF16)
            lb = hg_lb.reshape(2 * depth, -1)
            lb_spec = pl.BlockSpec(lb.shape, lambda bi, i: (0, 0))
            outs = ([("heads", HG_DK, BF16)] * 3 + [("heads", HG_DK, F32)] * 2
                    + [("heads", HG_DV, BF16), ("plain", d, BF16)])
            kern = functools.partial(_hg_proj_kernel, layer_idx=l, depth=depth)
            pp = _proj_call(kern, xp, mod_p, gpre, w_in, [lb], [lb_spec], outs, "hg_proj_prompt", tm)
            ps = _proj_call(kern, xs, mod_s, gpre, w_in, [lb], [lb_spec], outs, "hg_proj_sample", tm)
            gp, gs = pp[6], ps[6]
            op, st = _gla(*pp[:6], hg_norm[j], None, True)
            (os_,) = _gla(*ps[:6], hg_norm[j], state_hgrn[:, j], False)
            new_h.append(st)
        xp = _out_call(op, gp, xp, w_out, mod_p, gpost, tm)
        xs = _out_call(os_, gs, xs, w_out, mod_s, gpost, tm)

    bp = x_prompt.shape[0]
    new_cache_k = jnp.stack(new_k, axis=1).reshape(bp, len(new_k), tp, 2 * N_HEADS, DA_HEAD_DIM)
    new_cache_v = jnp.stack(new_v, axis=1).reshape(bp, len(new_v), tp, N_HEADS, DA_V_DIM)
    new_state_ret = jnp.stack(new_r, axis=1)
    new_state_hgrn = jnp.stack(new_h, axis=1)
    return (xp, xs, new_cache_k, new_cache_v, new_state_ret, new_state_hgrn)
```

```python
import functools
import math

import jax
import jax.numpy as jnp
from jax import lax
from jax.experimental import pallas as pl
from jax.experimental.pallas import tpu as pltpu

F32 = jnp.float32
BF16 = jnp.bfloat16

EPS = 1e-6
LANES = 128
GRID_W = 64
ROPE_BASE = 10000.0
N_HEADS = 8
DA_HEAD_DIM = 64
DA_V_DIM = 128
RET_DK = 128
RET_DV = 256
HG_DK = 128
HG_DV = 128
RET_CHUNK = 128
GLA_CHUNK = 64
CHUNK_UNROLL = 8
VMEM_LIMIT = 56 * 1024 * 1024
Q_SCALE_LOG2 = DA_HEAD_DIM ** -0.5 * math.log2(math.e)


def _cparams(*sem):
    return pltpu.CompilerParams(dimension_semantics=sem, vmem_limit_bytes=VMEM_LIMIT)


def _dot(a, b):
    return jnp.dot(a, b, preferred_element_type=F32)


def _dot_nt(a, b):
    return lax.dot_general(a, b, (((1,), (1,)), ((), ())), preferred_element_type=F32)


def _dot_tn(a, b):
    return lax.dot_general(a, b, (((0,), (0,)), ((), ())), preferred_element_type=F32)


def _silu(x):
    return x * jax.nn.sigmoid(x)


def _rms(x):
    return x * lax.rsqrt(jnp.mean(x * x, axis=-1, keepdims=True) + EPS)


def _mod_kernel(c_ref, w_ref, b_ref, o_ref):
    s = _silu(c_ref[...]).astype(BF16)
    o_ref[0] = _dot(s, w_ref[0].astype(BF16)) + b_ref[0]


def _modulation_all(cvec, w_mod, b_mod):
    depth, d, n = w_mod.shape
    r = cvec.shape[0]
    tn = 1024
    return pl.pallas_call(
        _mod_kernel,
        grid=(depth, n // tn),
        in_specs=[
            pl.BlockSpec((r, d), lambda l, j: (0, 0)),
            pl.BlockSpec((1, d, tn), lambda l, j: (l, 0, j)),
            pl.BlockSpec((1, 1, tn), lambda l, j: (l, 0, j)),
        ],
        out_specs=pl.BlockSpec((1, r, tn), lambda l, j: (l, 0, j)),
        out_shape=jax.ShapeDtypeStruct((depth, r, n), F32),
        compiler_params=_cparams("parallel", "parallel"),
        name="modulation",
    )(cvec, w_mod, b_mod.reshape(depth, 1, n))


PROJ_NC = 512


def _norm_mod(x_ref, gpre_ref, mod_ref):
    d = x_ref.shape[-1]
    xn = _rms(x_ref[0]) * gpre_ref[...]
    shift = mod_ref[0, :, 0:d]
    scale = mod_ref[0, :, d:2 * d]
    return (xn * (1.0 + scale) + shift).astype(BF16)


def _store_heads(ref, acc, col0, dh, dtype):
    for j in range(acc.shape[1] // dh):
        ref[0, col0 // dh + j] = acc[:, j * dh:(j + 1) * dh].astype(dtype)


def _da_proj_kernel(*refs, rope, emit_cache):
    x_ref, gpre_ref, mod_ref, w_ref = refs[:4]
    i = 4
    if rope:
        cos_ref, sinp_ref, sinn_ref = refs[i:i + 3]
        i += 3
    q_ref, k_ref, v_ref, g_ref = refs[i:i + 4]
    i += 4
    if emit_cache:
        kc_ref, vc_ref = refs[i:i + 2]
    d = x_ref.shape[-1]
    h = _norm_mod(x_ref, gpre_ref, mod_ref)

    def rot(a):
        out = []
        for j in range(a.shape[1] // LANES):
            t = a[:, j * LANES:(j + 1) * LANES]
            out.append(t * cos_ref[...] + pltpu.roll(t, 1, 1) * sinp_ref[...]
                       + pltpu.roll(t, LANES - 1, 1) * sinn_ref[...])
        return jnp.concatenate(out, axis=1)

    for sec in range(4):
        for c0 in range(0, d, PROJ_NC):
            acc = _dot(h, w_ref[:, sec * d + c0: sec * d + c0 + PROJ_NC])
            if sec == 0:
                if rope:
                    acc = rot(acc)
                _store_heads(q_ref, acc * Q_SCALE_LOG2, c0, LANES, BF16)
            elif sec == 1:
                if emit_cache:
                    kc_ref[0, :, c0:c0 + PROJ_NC] = acc
                if rope:
                    acc = rot(acc)
                _store_heads(k_ref, acc, c0, LANES, BF16)
            elif sec == 2:
                if emit_cache:
                    vc_ref[0, :, c0:c0 + PROJ_NC] = acc
                _store_heads(v_ref, acc, c0, DA_V_DIM, BF16)
            else:
                g_ref[0, :, c0:c0 + PROJ_NC] = acc.astype(BF16)


def _ret_proj_kernel(x_ref, gpre_ref, mod_ref, w_ref, q_ref, k_ref, v_ref, g_ref):
    d = x_ref.shape[-1]
    h = _norm_mod(x_ref, gpre_ref, mod_ref)
    qk = N_HEADS * RET_DK
    vw = N_HEADS * RET_DV
    for c0 in range(0, qk, PROJ_NC):
        _store_heads(q_ref, _dot(h, w_ref[:, c0:c0 + PROJ_NC]), c0, RET_DK, BF16)
    for c0 in range(0, qk, PROJ_NC):
        acc = _dot(h, w_ref[:, qk + c0: qk + c0 + PROJ_NC]) * (RET_DK ** -0.5)
        _store_heads(k_ref, acc, c0, RET_DK, BF16)
    for c0 in range(0, vw, PROJ_NC):
        _store_heads(v_ref, _dot(h, w_ref[:, 2 * qk + c0: 2 * qk + c0 + PROJ_NC]), c0, RET_DV, BF16)
    for c0 in range(0, vw, PROJ_NC):
        g_ref[0, :, c0:c0 + PROJ_NC] = _dot(
            h, w_ref[:, 2 * qk + vw + c0: 2 * qk + vw + c0 + PROJ_NC]).astype(BF16)


def _hg_proj_kernel(x_ref, gpre_ref, mod_ref, w_ref, lb_ref,
                    q_ref, kf_ref, kb_ref, gf_ref, gb_ref, v_ref, g_ref, *, layer_idx, depth):
    d = x_ref.shape[-1]
    h = _norm_mod(x_ref, gpre_ref, mod_ref)
    f_w = N_HEADS * HG_DK

    def lower_bound(direction):
        lg = lb_ref[direction * depth:(direction + 1) * depth, :]
        e = jnp.exp(lg - jnp.max(lg, axis=0, keepdims=True))
        sm = e / jnp.sum(e, axis=0, keepdims=True)
        cum = sm[0:1]
        for i in range(1, layer_idx + 1):
            cum = cum + sm[i:i + 1]
        return cum - sm[0:1]

    lbs = (lower_bound(0), lower_bound(1))
    for c0 in range(0, f_w, PROJ_NC):
        _store_heads(q_ref, _silu(_dot(h, w_ref[:, c0:c0 + PROJ_NC])), c0, HG_DK, BF16)
    for direction, (k_out, g_out) in enumerate(((kf_ref, gf_ref), (kb_ref, gb_ref))):
        base = (1 + direction) * f_w
        for c0 in range(0, f_w, PROJ_NC):
            acc = _dot(h, w_ref[:, base + c0: base + c0 + PROJ_NC])
            lbd = lbs[direction][:, c0:c0 + PROJ_NC]
            f = lbd + (1.0 - lbd) * jax.nn.sigmoid(acc)
            _store_heads(g_out, jnp.log(f), c0, HG_DK, F32)
            _store_heads(k_out, 1.0 - f, c0, HG_DK, BF16)
    for c0 in range(0, d, PROJ_NC):
        _store_heads(v_ref, _dot(h, w_ref[:, 3 * f_w + c0: 3 * f_w + c0 + PROJ_NC]), c0, HG_DV, BF16)
    for c0 in range(0, d, PROJ_NC):
        g_ref[0, :, c0:c0 + PROJ_NC] = _dot(
            h, w_ref[:, 3 * f_w + d + c0: 3 * f_w + d + c0 + PROJ_NC]).astype(BF16)


def _proj_call(kernel, x, mod, gpre, w, extra_in, extra_specs, outs, name, tm):
    b, t, d = x.shape
    n = w.shape[1]
    bm = mod.shape[0]
    mod_idx = (lambda bi, i: (bi, 0, 0)) if bm > 1 else (lambda bi, i: (0, 0, 0))
    in_specs = [
        pl.BlockSpec((1, tm, d), lambda bi, i: (bi, i, 0)),
        pl.BlockSpec((1, d), lambda bi, i: (0, 0)),
        pl.BlockSpec((1, 1, mod.shape[2]), mod_idx),
        pl.BlockSpec((d, n), lambda bi, i: (0, 0)),
    ] + extra_specs
    out_specs, out_shape = [], []
    for kind, width, dtype in outs:
        if kind == "heads":
            out_specs.append(pl.BlockSpec((1, N_HEADS, tm, width), lambda bi, i: (bi, 0, i, 0)))
            out_shape.append(jax.ShapeDtypeStruct((b, N_HEADS, t, width), dtype))
        else:
            out_specs.append(pl.BlockSpec((1, tm, width), lambda bi, i: (bi, i, 0)))
            out_shape.append(jax.ShapeDtypeStruct((b, t, width), dtype))
    return pl.pallas_call(
        kernel,
        grid=(b, t // tm),
        in_specs=in_specs,
        out_specs=out_specs,
        out_shape=out_shape,
        compiler_params=_cparams("parallel", "parallel"),
        name=name,
    )(x, gpre, mod, w, *extra_in)


def _attn_stages(lam_ref, subg_ref, q_ref, k_ref, v_ref, o_ref,
                 s_w, m_w, s_r, m_r, p_w, l_w, p_r, l_r, *, tq, kc, nkc, lam_init):
    q = q_ref[0, 0]
    lane = lax.broadcasted_iota(jnp.int32, (tq, LANES), 1)
    zero = jnp.zeros_like(q)
    qs = jnp.concatenate([jnp.where(lane < DA_HEAD_DIM, q, zero),
                          jnp.where(lane >= DA_HEAD_DIM, q, zero)], axis=0)
    m = jnp.full((2 * tq, LANES), -jnp.inf, F32)
    for c in range(nkc):
        s = _dot_nt(qs, k_ref[0, 0, c * kc:(c + 1) * kc, :])
        s_w[c] = s
        for j in range(kc // LANES):
            m = jnp.maximum(m, s[:, j * LANES:(j + 1) * LANES])
    m_w[...] = jnp.max(m, axis=-1, keepdims=True)

    m_row = m_r[...]
    l = jnp.zeros((2 * tq, LANES), F32)
    for c in range(nkc):
        p = jnp.exp2(s_r[c] - m_row)
        for j in range(kc // LANES):
            l = l + p[:, j * LANES:(j + 1) * LANES]
        p_w[:, c * kc:(c + 1) * kc] = p.astype(BF16)
    l_w[...] = jnp.sum(l, axis=-1, keepdims=True)

    on = _dot(p_r[...], v_ref[0, 0]) / l_r[...]
    lf = lam_ref[...]
    lam = (jnp.exp(jnp.sum(lf[0:1] * lf[1:2], axis=-1, keepdims=True))
           - jnp.exp(jnp.sum(lf[2:3] * lf[3:4], axis=-1, keepdims=True)) + lam_init)
    o = on[:tq] - lam * on[tq:]
    o = _rms(o) * subg_ref[...] * (1.0 - lam_init)
    o_ref[0, 0] = o.astype(BF16)


def _attn_kernel(lam_ref, subg_ref, q_ref, k_ref, v_ref, o_ref,
                 s0, s1, p0, p1, m0, m1, l0, l1, **static):
    t = pl.program_id(0)

    @pl.when(t == 0)
    def _init():
        s1[...] = jnp.zeros(s1.shape, F32)
        m1[...] = jnp.zeros(m1.shape, F32)
        p0[...] = jnp.zeros(p0.shape, BF16)
        p1[...] = jnp.zeros(p1.shape, BF16)
        l0[...] = jnp.ones(l0.shape, F32)
        l1[...] = jnp.ones(l1.shape, F32)

    io = (lam_ref, subg_ref, q_ref, k_ref, v_ref, o_ref)

    @pl.when(t % 2 == 0)
    def _even():
        _attn_stages(*io, s0, m0, s1, m1, p1, l1, p0, l0, **static)

    @pl.when(t % 2 == 1)
    def _odd():
        _attn_stages(*io, s1, m1, s0, m0, p0, l0, p1, l1, **static)


def _attention(q, k, v, lam_vec, sub_g, layer_idx, tq):
    b, nh, t, _ = q.shape
    tk = k.shape[2]
    kc = next(w for w in (512, 384, 256, 128) if tk % w == 0)
    nkc = tk // kc
    nq = t // tq
    n_units = b * nh * nq
    lam_init = 0.8 - 0.6 * math.exp(-0.3 * layer_idx)
    kern = functools.partial(_attn_kernel, tq=tq, kc=kc, nkc=nkc, lam_init=lam_init)

    def unit(u):
        return u // (nh * nq), (u // nq) % nh, u % nq

    def head_unit(step):
        return unit(jnp.minimum(step, n_units - 1))

    def tail_unit(step):
        return unit(jnp.clip(step - 2, 0, n_units - 1))

    def q_idx(step):
        bi, h, i = head_unit(step)
        return bi, h, i, 0

    def k_idx(step):
        bi, h, _ = head_unit(step)
        return bi, h, 0, 0

    def v_idx(step):
        bi, h, _ = tail_unit(step)
        return bi, h, 0, 0

    def o_idx(step):
        bi, h, i = tail_unit(step)
        return bi, h, i, 0

    return pl.pallas_call(
        kern,
        grid=(n_units + 2,),
        in_specs=[
            pl.BlockSpec(lam_vec.shape, lambda step: (0, 0)),
            pl.BlockSpec((1, DA_V_DIM), lambda step: (0, 0)),
            pl.BlockSpec((1, 1, tq, LANES), q_idx),
            pl.BlockSpec((1, 1, tk, LANES), k_idx),
            pl.BlockSpec((1, 1, tk, DA_V_DIM), v_idx),
        ],
        out_specs=pl.BlockSpec((1, 1, tq, DA_V_DIM), o_idx),
        out_shape=jax.ShapeDtypeStruct((b, nh, t, DA_V_DIM), BF16),
        scratch_shapes=(
            [pltpu.VMEM((nkc, 2 * tq, kc), F32)] * 2 + [pltpu.VMEM((2 * tq, tk), BF16)] * 2
            + [pltpu.VMEM((2 * tq, 1), F32)] * 4),
        compiler_params=_cparams("arbitrary"),
        name="diff_attention",
    )(lam_vec, sub_g.reshape(1, DA_V_DIM), q, k, v)


def _ret_kernel(*refs, chunk, nc, has_s0, emit_state):
    dec_ref, q_ref, k_ref, v_ref = refs[:4]
    i = 4
    if has_s0:
        s0_ref = refs[i]
        i += 1
    o_ref = refs[i]
    i += 1
    if emit_state:
        st_ref = refs[i]
        i += 1
    sf_ref, sb_ref = refs[i:i + 2]
    c_ = chunk
    lg = jnp.log1p(-jnp.exp(dec_ref[0]))
    lgf, lgb = lg[0:1, :RET_DK], lg[1:2, :RET_DK]
    row = lax.broadcasted_iota(jnp.int32, (c_, RET_DK), 0).astype(F32)
    k_dec_f = jnp.exp((c_ - 1.0 - row) * lgf)
    k_dec_b = jnp.exp(row * lgb)
    q_dec_f = jnp.exp((row + 1.0) * lgf)
    q_dec_b = jnp.exp((c_ - row) * lgb)
    chunk_dec_f = jnp.exp(c_ * lg[0:1])
    chunk_dec_b = jnp.exp(c_ * lg[1:2])
    diff = row - lax.broadcasted_iota(jnp.int32, (c_, c_), 1).astype(F32)
    dmat = (jnp.where(diff >= 0, jnp.exp(jnp.where(diff >= 0, diff, 0.0) * lgf), 0.0)
            + jnp.where(diff <= 0, jnp.exp(jnp.where(diff <= 0, -diff, 0.0) * lgb), 0.0))

    def rows(c):
        return pl.ds(pl.multiple_of(c * c_, c_), c_)

    def chunk_kv(c, carry):
        kc = k_ref[0, 0, rows(c), :].astype(F32)
        vc = v_ref[0, 0, rows(c), :]
        kd = jnp.concatenate([(kc * k_dec_f).astype(BF16), (kc * k_dec_b).astype(BF16)], axis=1)
        u = _dot_tn(kd, vc)
        sf_ref[c] = u[:RET_DK]
        sb_ref[c] = u[RET_DK:]
        return carry

    lax.fori_loop(0, nc, chunk_kv, 0, unroll=min(nc, CHUNK_UNROLL))

    if has_s0:
        s_f, s_b = s0_ref[0, 0, 0], s0_ref[0, 1, 0]
    else:
        s_f = s_b = jnp.zeros((RET_DK, RET_DV), F32)

    def scan_f(c, s):
        u = sf_ref[c]
        sf_ref[c] = s
        return chunk_dec_f * s + u

    def scan_b(j, s):
        c = nc - 1 - j
        u = sb_ref[c]
        sb_ref[c] = s
        return chunk_dec_b * s + u

    s_f = lax.fori_loop(0, nc, scan_f, s_f)
    s_b = lax.fori_loop(0, nc, scan_b, s_b)
    if emit_state:
        st_ref[0, 0, 0] = s_f
        st_ref[0, 1, 0] = s_b

    def chunk_out(c, carry):
        qc = q_ref[0, 0, rows(c), :]
        kc = k_ref[0, 0, rows(c), :]
        vc = v_ref[0, 0, rows(c), :]
        sc = (_dot_nt(qc, kc) * dmat).astype(BF16)
        qf = qc.astype(F32)
        lhs = jnp.concatenate([sc, (qf * q_dec_f).astype(BF16), (qf * q_dec_b).astype(BF16)], axis=1)
        rhs = jnp.concatenate([vc, sf_ref[c].astype(BF16), sb_ref[c].astype(BF16)], axis=0)
        o = _dot(lhs, rhs)
        o_ref[0, 0, rows(c), :] = _rms(o).astype(BF16)
        return carry

    lax.fori_loop(0, nc, chunk_out, 0, unroll=min(nc, CHUNK_UNROLL))


def _retention(q, k, v, dec, s0, emit_state):
    b, nh, t, _ = q.shape
    c_ = RET_CHUNK
    nc = t // c_
    has_s0 = s0 is not None
    kern = functools.partial(_ret_kernel, chunk=c_, nc=nc, has_s0=has_s0, emit_state=emit_state)
    in_specs = [
        pl.BlockSpec((1, 2, RET_DV), lambda bi, h: (h, 0, 0)),
        pl.BlockSpec((1, 1, t, RET_DK), lambda bi, h: (bi, h, 0, 0)),
        pl.BlockSpec((1, 1, t, RET_DK), lambda bi, h: (bi, h, 0, 0)),
        pl.BlockSpec((1, 1, t, RET_DV), lambda bi, h: (bi, h, 0, 0)),
    ]
    args = [dec, q, k, v]
    st_spec = pl.BlockSpec((1, 2, 1, RET_DK, RET_DV), lambda bi, h: (bi, 0, h, 0, 0))
    if has_s0:
        in_specs.append(st_spec)
        args.append(s0)
    out_specs = [pl.BlockSpec((1, 1, t, RET_DV), lambda bi, h: (bi, h, 0, 0))]
    out_shape = [jax.ShapeDtypeStruct((b, nh, t, RET_DV), BF16)]
    if emit_state:
        out_specs.append(st_spec)
        out_shape.append(jax.ShapeDtypeStruct((b, 2, nh, RET_DK, RET_DV), F32))
    return pl.pallas_call(
        kern,
        grid=(b, nh),
        in_specs=in_specs,
        out_specs=out_specs,
        out_shape=out_shape,
        scratch_shapes=[pltpu.VMEM((nc, RET_DK, RET_DV), F32)] * 2,
        compiler_params=_cparams("parallel", "parallel"),
        name="retention",
    )(*args)


def _gla_kernel(*refs, chunk, nc, has_s0, emit_state):
    ng_ref, q_ref, kf_ref, kb_ref, gf_ref, gb_ref, v_ref = refs[:7]
    i = 7
    if has_s0:
        s0_ref = refs[i]
        i += 1
    o_ref = refs[i]
    i += 1
    if emit_state:
        st_ref = refs[i]
        i += 1
    sf_ref, sb_ref, bf_ref, bb_ref = refs[i:i + 4]
    c_ = chunk
    mid = c_ // 2
    r = lax.broadcasted_iota(jnp.int32, (c_, c_), 0)
    cc = lax.broadcasted_iota(jnp.int32, (c_, c_), 1)
    tri_f = (r >= cc).astype(BF16)
    tri_b = (r <= cc).astype(BF16)
    r2 = lax.broadcasted_iota(jnp.int32, (c_, 2 * c_), 0)
    c2 = lax.broadcasted_iota(jnp.int32, (c_, 2 * c_), 1)
    mask_f = (c2 < c_) & (r2 >= c2)
    mask_b = (c2 >= c_) & (r2 <= c2 - c_)

    def rows(c):
        return pl.ds(pl.multiple_of(c * c_, c_), c_)

    group = min(nc, 8)
    for g_ref, tri, b_ref in ((gf_ref, tri_f, bf_ref), (gb_ref, tri_b, bb_ref)):
        for c0 in range(0, nc, group):
            g = jnp.concatenate([g_ref[0, 0, (c0 + j) * c_:(c0 + j + 1) * c_, :] for j in range(group)], axis=1)
            hi = g.astype(BF16)
            lo = (g - hi.astype(F32)).astype(BF16)
            b = _dot(tri, hi) + _dot(tri, lo)
            for j in range(group):
                b_ref[(c0 + j) * c_:(c0 + j + 1) * c_, :] = b[:, j * HG_DK:(j + 1) * HG_DK]

    def chunk_kv(c, carry):
        bf = bf_ref[rows(c), :]
        bb = bb_ref[rows(c), :]
        kdf = (kf_ref[0, 0, rows(c), :].astype(F32) * jnp.exp(bf[c_ - 1:c_] - bf)).astype(BF16)
        kdb = (kb_ref[0, 0, rows(c), :].astype(F32) * jnp.exp(bb[0:1] - bb)).astype(BF16)
        u = _dot_tn(v_ref[0, 0, rows(c), :], jnp.concatenate([kdf, kdb], axis=1))
        sf_ref[c] = u[:, :HG_DK]
        sb_ref[c] = u[:, HG_DK:]
        return carry

    lax.fori_loop(0, nc, chunk_kv, 0, unroll=min(nc, CHUNK_UNROLL))

    if has_s0:
        s_f, s_b = s0_ref[0, 0, 0].T, s0_ref[0, 1, 0].T
    else:
        s_f = s_b = jnp.zeros((HG_DV, HG_DK), F32)

    def scan_f(c, s):
        u = sf_ref[c]
        sf_ref[c] = s
        last = pl.multiple_of(c * c_, c_) + (c_ - 1)
        return jnp.exp(bf_ref[pl.ds(last, 1), :]) * s + u

    def scan_b(j, s):
        c = nc - 1 - j
        u = sb_ref[c]
        sb_ref[c] = s
        first = pl.multiple_of(c * c_, c_)
        return jnp.exp(bb_ref[pl.ds(first, 1), :]) * s + u

    s_f = lax.fori_loop(0, nc, scan_f, s_f)
    s_b = lax.fori_loop(0, nc, scan_b, s_b)
    if emit_state:
        st_ref[0, 0, 0] = s_f.T
        st_ref[0, 1, 0] = s_b.T

    def chunk_out(c, carry):
        q = q_ref[0, 0, rows(c), :].astype(F32)
        kf = kf_ref[0, 0, rows(c), :].astype(F32)
        kb = kb_ref[0, 0, rows(c), :].astype(F32)
        v = v_ref[0, 0, rows(c), :]
        bf = bf_ref[rows(c), :]
        bb = bb_ref[rows(c), :]
        ref_f = bf[mid:mid + 1]
        ref_b = bb[c_ - 1 - mid:c_ - mid]
        qq = jnp.concatenate([(q * jnp.exp(bf - ref_f)).astype(BF16), (q * jnp.exp(bb - ref_b)).astype(BF16)], axis=0)
        kk = jnp.concatenate([(kf * jnp.exp(ref_f - bf)).astype(BF16), (kb * jnp.exp(ref_b - bb)).astype(BF16)], axis=0)
        pr = _dot_nt(qq, kk)
        sc = jnp.where(mask_f, pr[:c_], 0.0) + jnp.where(mask_b, pr[c_:], 0.0)
        qcat = jnp.concatenate([(q * jnp.exp(bf)).astype(BF16), (q * jnp.exp(bb)).astype(BF16)], axis=1)
        scat = jnp.concatenate([sf_ref[c], sb_ref[c]], axis=1).astype(BF16)
        o = _dot(sc.astype(BF16), jnp.concatenate([v, v], axis=0)) + _dot_nt(qcat, scat)
        o_ref[0, 0, rows(c), :] = (_rms(o) * ng_ref[...]).astype(BF16)
        return carry

    lax.fori_loop(0, nc, chunk_out, 0, unroll=min(nc, CHUNK_UNROLL))


def _gla(q, kf, kb, gf, gb, v, norm_g, s0, emit_state):
    b, nh, t, _ = q.shape
    c_ = GLA_CHUNK
    nc = t // c_
    has_s0 = s0 is not None
    kern = functools.partial(_gla_kernel, chunk=c_, nc=nc, has_s0=has_s0, emit_state=emit_state)
    seq = pl.BlockSpec((1, 1, t, HG_DK), lambda bi, h: (bi, h, 0, 0))
    in_specs = [pl.BlockSpec((1, HG_DV), lambda bi, h: (0, 0))] + [seq] * 6
    args = [norm_g.reshape(1, HG_DV), q, kf, kb, gf, gb, v]
    st_spec = pl.BlockSpec((1, 2, 1, HG_DK, HG_DV), lambda bi, h: (bi, 0, h, 0, 0))
    if has_s0:
        in_specs.append(st_spec)
        args.append(s0)
    out_specs = [pl.BlockSpec((1, 1, t, HG_DV), lambda bi, h: (bi, h, 0, 0))]
    out_shape = [jax.ShapeDtypeStruct((b, nh, t, HG_DV), BF16)]
    if emit_state:
        out_specs.append(st_spec)
        out_shape.append(jax.ShapeDtypeStruct((b, 2, nh, HG_DK, HG_DV), F32))
    return pl.pallas_call(
        kern,
        grid=(b, nh),
        in_specs=in_specs,
        out_specs=out_specs,
        out_shape=out_shape,
        scratch_shapes=[pltpu.VMEM((nc, HG_DV, HG_DK), F32)] * 2 + [pltpu.VMEM((t, HG_DK), F32)] * 2,
        compiler_params=_cparams("parallel", "parallel"),
        name="gla",
    )(*args)


def _out_kernel(o_ref, g_ref, x_ref, w_ref, mod_ref, gpost_ref, y_ref):
    d = x_ref.shape[-1]
    o = jnp.concatenate([o_ref[0, h] for h in range(o_ref.shape[1])], axis=-1).astype(F32)
    z = (o * _silu(g_ref[0].astype(F32))).astype(BF16)
    y = _rms(_dot(z, w_ref[...])) * gpost_ref[...]
    y_ref[0] = x_ref[0] + mod_ref[0, :, 2 * d:3 * d] * y


def _out_call(o, g, x, w, mod, gpost, tm):
    b, t, d = x.shape
    nh, dv = o.shape[1], o.shape[3]
    wd = nh * dv
    bm = mod.shape[0]
    mod_idx = (lambda bi, i: (bi, 0, 0)) if bm > 1 else (lambda bi, i: (0, 0, 0))
    return pl.pallas_call(
        _out_kernel,
        grid=(b, t // tm),
        in_specs=[
            pl.BlockSpec((1, nh, tm, dv), lambda bi, i: (bi, 0, i, 0)),
            pl.BlockSpec((1, tm, wd), lambda bi, i: (bi, i, 0)),
            pl.BlockSpec((1, tm, d), lambda bi, i: (bi, i, 0)),
            pl.BlockSpec((wd, d), lambda bi, i: (0, 0)),
            pl.BlockSpec((1, 1, mod.shape[2]), mod_idx),
            pl.BlockSpec((1, d), lambda bi, i: (0, 0)),
        ],
        out_specs=pl.BlockSpec((1, tm, d), lambda bi, i: (bi, i, 0)),
        out_shape=jax.ShapeDtypeStruct((b, t, d), F32),
        compiler_params=_cparams("parallel", "parallel"),
        name="out_proj",
    )(o, g, x, w, mod, gpost)


def _rope_tables(t):
    rows = t // GRID_W
    row = jnp.repeat(jnp.arange(rows), GRID_W)
    col = jnp.broadcast_to(jnp.arange(GRID_W), (rows, GRID_W)).reshape(-1)
    n_pair = DA_HEAD_DIM // 4
    inv = ROPE_BASE ** (-jnp.arange(n_pair, dtype=F32) / n_pair)
    ang = jnp.concatenate([row[:, None] * inv, col[:, None] * inv], axis=-1)
    cos = jnp.repeat(jnp.cos(ang), 2, axis=-1)
    sin = jnp.repeat(jnp.sin(ang), 2, axis=-1)
    odd = (jnp.arange(DA_HEAD_DIM) % 2) == 1
    sin_prev = jnp.where(odd, sin, 0.0)
    sin_next = jnp.where(odd, 0.0, -sin)
    tile = lambda a: jnp.tile(a, (1, LANES // DA_HEAD_DIM))
    return tile(cos), tile(sin_prev), tile(sin_next)


def kernel(x_prompt, x_sample, cache_k, cache_v, state_ret, state_hgrn, c, c_ctx,
           w_mod, b_mod, g_pre, g_post,
           da_w_in, da_w_out, da_lambda, da_subln,
           ret_w_in, ret_w_out, ret_decay,
           hg_w_in, hg_w_out, hg_lb, hg_norm):
    depth, d, _ = w_mod.shape
    db = x_sample.shape[0]
    ts = x_sample.shape[1]
    tp = x_prompt.shape[1]
    n_mix = 3
    kinds = [l % n_mix for l in range(depth)]
    slots = [kinds[:l].count(kinds[l]) for l in range(depth)]

    cvec = jnp.zeros((16, d), F32).at[:db].set(c).at[db].set(c_ctx)
    mods = _modulation_all(cvec, w_mod, b_mod)

    rope = _rope_tables(ts)
    tab_spec = pl.BlockSpec((256, LANES), lambda bi, i: (i, 0))
    tm = 256

    xp, xs = x_prompt, x_sample
    new_k, new_v, new_r, new_h = [], [], [], []
    for l in range(depth):
        kind, j = kinds[l], slots[l]
        mod_p = mods[l, db:db + 1].reshape(1, 1, 3 * d)
        mod_s = mods[l, :db].reshape(db, 1, 3 * d)
        gpre = g_pre[l].reshape(1, d)
        gpost = g_post[l].reshape(1, d)
        if kind == 0:
            w_in = da_w_in[j].astype(BF16)
            w_out = da_w_out[j].astype(BF16)
            heads = [("heads", LANES, BF16)] * 3 + [("plain", d, BF16)]
            qp, kp, vp, gp, kcache, vcache = _proj_call(
                functools.partial(_da_proj_kernel, rope=False, emit_cache=True),
                xp, mod_p, gpre, w_in, [], [], heads + [("plain", d, F32)] * 2, "da_proj_prompt", tm)
            qs, ks, vs, gs = _proj_call(
                functools.partial(_da_proj_kernel, rope=True, emit_cache=False),
                xs, mod_s, gpre, w_in, list(rope), [tab_spec] * 3, heads, "da_proj_sample", tm)
            new_k.append(kcache)
            new_v.append(vcache)
            ck = cache_k[:, j].reshape(db, -1, N_HEADS, LANES).transpose(0, 2, 1, 3).astype(BF16)
            cv = cache_v[:, j].transpose(0, 2, 1, 3).astype(BF16)
            ks = jnp.concatenate([ks, ck], axis=2)
            vs = jnp.concatenate([vs, cv], axis=2)
            op = _attention(qp, kp, vp, da_lambda[j], da_subln[j], l, tq=256)
            os_ = _attention(qs, ks, vs, da_lambda[j], da_subln[j], l, tq=256)
        elif kind == 1:
            w_in = ret_w_in[j].astype(BF16)
            w_out = ret_w_out[j].astype(BF16)
            outs = [("heads", RET_DK, BF16)] * 2 + [("heads", RET_DV, BF16), ("plain", N_HEADS * RET_DV, BF16)]
            qp, kp, vp, gp = _proj_call(_ret_proj_kernel, xp, mod_p, gpre, w_in, [], [], outs,
                                        "ret_proj_prompt", tm)
            qs, ks, vs, gs = _proj_call(_ret_proj_kernel, xs, mod_s, gpre, w_in, [], [], outs,
                                        "ret_proj_sample", tm)
            dec = jnp.broadcast_to(ret_decay[j].T[:, :, None], (N_HEADS, 2, RET_DV))
            op, st = _retention(qp, kp, vp, dec, None, True)
            (os_,) = _retention(qs, ks, vs, dec, state_ret[:, j], False)
            new_r.append(st)
        else:
            w_in = hg_w_in[j].astype(BF16)
            w_out = hg_w_out[j].astype(BF16)
            lb = hg_lb.reshape(2 * depth, -1)
            lb_spec = pl.BlockSpec(lb.shape, lambda bi, i: (0, 0))
            outs = ([("heads", HG_DK, BF16)] * 3 + [("heads", HG_DK, F32)] * 2
                    + [("heads", HG_DV, BF16), ("plain", d, BF16)])
            kern = functools.partial(_hg_proj_kernel, layer_idx=l, depth=depth)
            pp = _proj_call(kern, xp, mod_p, gpre, w_in, [lb], [lb_spec], outs, "hg_proj_prompt", tm)
            ps = _proj_call(kern, xs, mod_s, gpre, w_in, [lb], [lb_spec], outs, "hg_proj_sample", tm)
            gp, gs = pp[6], ps[6]
            op, st = _gla(*pp[:6], hg_norm[j], None, True)
            (os_,) = _gla(*ps[:6], hg_norm[j], state_hgrn[:, j], False)
            new_h.append(st)
        xp = _out_call(op, gp, xp, w_out, mod_p, gpost, tm)
        xs = _out_call(os_, gs, xs, w_out, mod_s, gpost, tm)

    bp = x_prompt.shape[0]
    new_cache_k = jnp.stack(new_k, axis=1).reshape(bp, len(new_k), tp, 2 * N_HEADS, DA_HEAD_DIM)
    new_cache_v = jnp.stack(new_v, axis=1).reshape(bp, len(new_v), tp, N_HEADS, DA_V_DIM)
    new_state_ret = jnp.stack(new_r, axis=1)
    new_state_hgrn = jnp.stack(new_h, axis=1)
    return (xp, xs, new_cache_k, new_cache_v, new_state_ret, new_state_hgrn)
```

```python
import functools
import math

import jax
import jax.numpy as jnp
from jax import lax
from jax.experimental import pallas as pl
from jax.experimental.pallas import tpu as pltpu

F32 = jnp.float32
BF16 = jnp.bfloat16

EPS = 1e-6
LANES = 128
GRID_W = 64
ROPE_BASE = 10000.0
N_HEADS = 8
DA_HEAD_DIM = 64
DA_V_DIM = 128
RET_DK = 128
RET_DV = 256
HG_DK = 128
HG_DV = 128
RET_CHUNK = 128
GLA_CHUNK = 64
CHUNK_UNROLL = 8
VMEM_LIMIT = 56 * 1024 * 1024
Q_SCALE_LOG2 = DA_HEAD_DIM ** -0.5 * math.log2(math.e)


def _cparams(*sem):
    return pltpu.CompilerParams(dimension_semantics=sem, vmem_limit_bytes=VMEM_LIMIT)


def _dot(a, b):
    return jnp.dot(a, b, preferred_element_type=F32)


def _dot_nt(a, b):
    return lax.dot_general(a, b, (((1,), (1,)), ((), ())), preferred_element_type=F32)


def _dot_tn(a, b):
    return lax.dot_general(a, b, (((0,), (0,)), ((), ())), preferred_element_type=F32)


def _silu(x):
    return x * jax.nn.sigmoid(x)


def _rms(x):
    return x * lax.rsqrt(jnp.mean(x * x, axis=-1, keepdims=True) + EPS)


def _mod_kernel(c_ref, w_ref, b_ref, o_ref):
    s = _silu(c_ref[...]).astype(BF16)
    o_ref[0] = _dot(s, w_ref[0].astype(BF16)) + b_ref[0]


def _modulation_all(cvec, w_mod, b_mod):
    depth, d, n = w_mod.shape
    r = cvec.shape[0]
    tn = 1024
    return pl.pallas_call(
        _mod_kernel,
        grid=(depth, n // tn),
        in_specs=[
            pl.BlockSpec((r, d), lambda l, j: (0, 0)),
            pl.BlockSpec((1, d, tn), lambda l, j: (l, 0, j)),
            pl.BlockSpec((1, 1, tn), lambda l, j: (l, 0, j)),
        ],
        out_specs=pl.BlockSpec((1, r, tn), lambda l, j: (l, 0, j)),
        out_shape=jax.ShapeDtypeStruct((depth, r, n), F32),
        compiler_params=_cparams("parallel", "parallel"),
        name="modulation",
    )(cvec, w_mod, b_mod.reshape(depth, 1, n))


PROJ_NC = 512


def _norm_mod(x_ref, gpre_ref, mod_ref):
    d = x_ref.shape[-1]
    xn = _rms(x_ref[0]) * gpre_ref[...]
    shift = mod_ref[0, :, 0:d]
    scale = mod_ref[0, :, d:2 * d]
    return (xn * (1.0 + scale) + shift).astype(BF16)


def _store_heads(ref, acc, col0, dh, dtype):
    for j in range(acc.shape[1] // dh):
        ref[0, col0 // dh + j] = acc[:, j * dh:(j + 1) * dh].astype(dtype)


def _da_proj_kernel(*refs, rope, emit_cache, cache_slot=0, n_carried=0):
    x_ref, gpre_ref, mod_ref, w_ref = refs[:4]
    i = 4
    if rope:
        cos_ref, sinp_ref, sinn_ref = refs[i:i + 3]
        i += 3
    i += n_carried
    q_ref, k_ref, v_ref, g_ref = refs[i:i + 4]
    i += 4
    d = x_ref.shape[-1]
    if emit_cache:
        kc_ref, vc_ref = refs[i:i + 2]
        for ref in (kc_ref, vc_ref):
            for s in range(ref.shape[1]):
                if s != cache_slot:
                    ref[0, s] = jnp.zeros(ref.shape[2:], ref.dtype)
    h = _norm_mod(x_ref, gpre_ref, mod_ref)

    def rot(a):
        out = []
        for j in range(a.shape[1] // LANES):
            t = a[:, j * LANES:(j + 1) * LANES]
            out.append(t * cos_ref[...] + pltpu.roll(t, 1, 1) * sinp_ref[...]
                       + pltpu.roll(t, LANES - 1, 1) * sinn_ref[...])
        return jnp.concatenate(out, axis=1)

    for sec in range(4):
        for c0 in range(0, d, PROJ_NC):
            acc = _dot(h, w_ref[:, sec * d + c0: sec * d + c0 + PROJ_NC])
            if sec == 0:
                if rope:
                    acc = rot(acc)
                _store_heads(q_ref, acc * Q_SCALE_LOG2, c0, LANES, BF16)
            elif sec == 1:
                if emit_cache:
                    kc_ref[0, cache_slot, :, c0:c0 + PROJ_NC] = acc
                if rope:
                    acc = rot(acc)
                _store_heads(k_ref, acc, c0, LANES, BF16)
            elif sec == 2:
                if emit_cache:
                    vc_ref[0, cache_slot, :, c0:c0 + PROJ_NC] = acc
                _store_heads(v_ref, acc, c0, DA_V_DIM, BF16)
            else:
                g_ref[0, :, c0:c0 + PROJ_NC] = acc.astype(BF16)


def _ret_proj_kernel(x_ref, gpre_ref, mod_ref, w_ref, q_ref, k_ref, v_ref, g_ref):
    d = x_ref.shape[-1]
    h = _norm_mod(x_ref, gpre_ref, mod_ref)
    qk = N_HEADS * RET_DK
    vw = N_HEADS * RET_DV
    for c0 in range(0, qk, PROJ_NC):
        _store_heads(q_ref, _dot(h, w_ref[:, c0:c0 + PROJ_NC]), c0, RET_DK, BF16)
    for c0 in range(0, qk, PROJ_NC):
        acc = _dot(h, w_ref[:, qk + c0: qk + c0 + PROJ_NC]) * (RET_DK ** -0.5)
        _store_heads(k_ref, acc, c0, RET_DK, BF16)
    for c0 in range(0, vw, PROJ_NC):
        _store_heads(v_ref, _dot(h, w_ref[:, 2 * qk + c0: 2 * qk + c0 + PROJ_NC]), c0, RET_DV, BF16)
    for c0 in range(0, vw, PROJ_NC):
        g_ref[0, :, c0:c0 + PROJ_NC] = _dot(
            h, w_ref[:, 2 * qk + vw + c0: 2 * qk + vw + c0 + PROJ_NC]).astype(BF16)


def _hg_proj_kernel(x_ref, gpre_ref, mod_ref, w_ref, lb_ref,
                    q_ref, kf_ref, kb_ref, gf_ref, gb_ref, v_ref, g_ref, *, layer_idx, depth):
    d = x_ref.shape[-1]
    h = _norm_mod(x_ref, gpre_ref, mod_ref)
    f_w = N_HEADS * HG_DK

    def lower_bound(direction):
        lg = lb_ref[direction * depth:(direction + 1) * depth, :]
        e = jnp.exp(lg - jnp.max(lg, axis=0, keepdims=True))
        sm = e / jnp.sum(e, axis=0, keepdims=True)
        cum = sm[0:1]
        for i in range(1, layer_idx + 1):
            cum = cum + sm[i:i + 1]
        return cum - sm[0:1]

    lbs = (lower_bound(0), lower_bound(1))
    for c0 in range(0, f_w, PROJ_NC):
        _store_heads(q_ref, _silu(_dot(h, w_ref[:, c0:c0 + PROJ_NC])), c0, HG_DK, BF16)
    for direction, (k_out, g_out) in enumerate(((kf_ref, gf_ref), (kb_ref, gb_ref))):
        base = (1 + direction) * f_w
        for c0 in range(0, f_w, PROJ_NC):
            acc = _dot(h, w_ref[:, base + c0: base + c0 + PROJ_NC])
            lbd = lbs[direction][:, c0:c0 + PROJ_NC]
            f = lbd + (1.0 - lbd) * jax.nn.sigmoid(acc)
            _store_heads(g_out, jnp.log(f), c0, HG_DK, F32)
            _store_heads(k_out, 1.0 - f, c0, HG_DK, BF16)
    for c0 in range(0, d, PROJ_NC):
        _store_heads(v_ref, _dot(h, w_ref[:, 3 * f_w + c0: 3 * f_w + c0 + PROJ_NC]), c0, HG_DV, BF16)
    for c0 in range(0, d, PROJ_NC):
        g_ref[0, :, c0:c0 + PROJ_NC] = _dot(
            h, w_ref[:, 3 * f_w + d + c0: 3 * f_w + d + c0 + PROJ_NC]).astype(BF16)


def _proj_call(kernel, x, mod, gpre, w, extra_in, extra_specs, outs, name, tm):
    b, t, d = x.shape
    n = w.shape[1]
    bm = mod.shape[0]
    mod_idx = (lambda bi, i: (bi, 0, 0)) if bm > 1 else (lambda bi, i: (0, 0, 0))
    in_specs = [
        pl.BlockSpec((1, tm, d), lambda bi, i: (bi, i, 0)),
        pl.BlockSpec((1, d), lambda bi, i: (0, 0)),
        pl.BlockSpec((1, 1, mod.shape[2]), mod_idx),
        pl.BlockSpec((d, n), lambda bi, i: (0, 0)),
    ] + extra_specs
    args = [x, gpre, mod, w, *extra_in]
    out_specs, out_shape, aliases = [], [], {}
    for out in outs:
        kind, width, dtype = out[:3]
        if kind == "heads":
            out_specs.append(pl.BlockSpec((1, N_HEADS, tm, width), lambda bi, i: (bi, 0, i, 0)))
            out_shape.append(jax.ShapeDtypeStruct((b, N_HEADS, t, width), dtype))
        elif kind == "plain":
            out_specs.append(pl.BlockSpec((1, tm, width), lambda bi, i: (bi, i, 0)))
            out_shape.append(jax.ShapeDtypeStruct((b, t, width), dtype))
        else:
            n_slots, slot, carried = out[3:]
            out_shape.append(jax.ShapeDtypeStruct((b, n_slots, t, width), dtype))
            if carried is None:
                out_specs.append(pl.BlockSpec((1, n_slots, tm, width), lambda bi, i: (bi, 0, i, 0)))
            else:
                out_specs.append(pl.BlockSpec((1, 1, tm, width), functools.partial(lambda bi, i, s: (bi, s, i, 0), s=slot)))
                aliases[len(args)] = len(out_shape) - 1
                in_specs.append(pl.BlockSpec(memory_space=pl.ANY))
                args.append(carried)
    return pl.pallas_call(
        kernel,
        grid=(b, t // tm),
        in_specs=in_specs,
        out_specs=out_specs,
        out_shape=out_shape,
        input_output_aliases=aliases,
        compiler_params=_cparams("parallel", "parallel"),
        name=name,
    )(*args)


def _attn_stages(lam_ref, subg_ref, q_ref, k_ref, v_ref, ck_ref, cv_ref, o_ref, s_w, m_w, s_r, m_r,
                 *, tq, kc, nkc, nkc_own, lam_init):
    def chunk_of(own_ref, cache_ref, c):
        if c < nkc_own:
            return own_ref[0, 0, c * kc:(c + 1) * kc, :]
        return cache_ref[0, 0, (c - nkc_own) * kc:(c - nkc_own + 1) * kc, :].astype(BF16)

    q = q_ref[0, 0]
    lane = lax.broadcasted_iota(jnp.int32, (tq, LANES), 1)
    zero = jnp.zeros_like(q)
    qs = jnp.concatenate([jnp.where(lane < DA_HEAD_DIM, q, zero),
                          jnp.where(lane >= DA_HEAD_DIM, q, zero)], axis=0)
    m = jnp.full((2 * tq, LANES), -jnp.inf, F32)
    for c in range(nkc):
        s = _dot_nt(qs, chunk_of(k_ref, ck_ref, c))
        s_w[c] = s
        for j in range(kc // LANES):
            m = jnp.maximum(m, s[:, j * LANES:(j + 1) * LANES])
    m_w[...] = jnp.max(m, axis=-1, keepdims=True)

    m_row = m_r[...]
    acc = jnp.zeros((2 * tq, 2 * DA_V_DIM), F32)
    for c in range(nkc):
        p = jnp.exp2(s_r[c] - m_row).astype(BF16)
        v = chunk_of(v_ref, cv_ref, c)
        acc = acc + _dot(p, jnp.concatenate([v, jnp.ones_like(v)], axis=1))
    on = acc[:, :DA_V_DIM] / acc[:, DA_V_DIM:]
    lf = lam_ref[...]
    lam = (jnp.exp(jnp.sum(lf[0:1] * lf[1:2], axis=-1, keepdims=True))
           - jnp.exp(jnp.sum(lf[2:3] * lf[3:4], axis=-1, keepdims=True)) + lam_init)
    o = on[:tq] - lam * on[tq:]
    o = _rms(o) * subg_ref[...] * (1.0 - lam_init)
    o_ref[0, 0] = o.astype(BF16)


def _attn_kernel(*refs, has_cache, **static):
    if has_cache:
        lam_ref, subg_ref, q_ref, k_ref, v_ref, ck_ref, cv_ref, o_ref, s0, s1, m0, m1 = refs
    else:
        lam_ref, subg_ref, q_ref, k_ref, v_ref, o_ref, s0, s1, m0, m1 = refs
        ck_ref = cv_ref = None
    t = pl.program_id(0)

    @pl.when(t == 0)
    def _init():
        s1[...] = jnp.zeros(s1.shape, F32)
        m1[...] = jnp.zeros(m1.shape, F32)

    io = (lam_ref, subg_ref, q_ref, k_ref, v_ref, ck_ref, cv_ref, o_ref)

    @pl.when(t % 2 == 0)
    def _even():
        _attn_stages(*io, s0, m0, s1, m1, **static)

    @pl.when(t % 2 == 1)
    def _odd():
        _attn_stages(*io, s1, m1, s0, m0, **static)


def _attention(q, k, v, lam_vec, sub_g, layer_idx, tq, cache=None):
    b, nh, t, _ = q.shape
    past = 0 if cache is None else cache[0].shape[2]
    kc = next(w for w in (512, 256, 128) if t % w == 0 and past % w == 0)
    nkc_own = t // kc
    nkc = nkc_own + past // kc
    nq = t // tq
    n_units = b * nh * nq
    lam_init = 0.8 - 0.6 * math.exp(-0.3 * layer_idx)
    kern = functools.partial(_attn_kernel, has_cache=cache is not None, tq=tq, kc=kc, nkc=nkc,
                             nkc_own=nkc_own, lam_init=lam_init)

    def unit(u):
        return u // (nh * nq), (u // nq) % nh, u % nq

    def head_unit(step):
        return unit(jnp.minimum(step, n_units - 1))

    def tail_unit(step):
        return unit(jnp.maximum(step - 1, 0))

    def q_idx(step):
        bi, h, i = head_unit(step)
        return bi, h, i, 0

    def k_idx(step):
        bi, h, _ = head_unit(step)
        return bi, h, 0, 0

    def v_idx(step):
        bi, h, _ = tail_unit(step)
        return bi, h, 0, 0

    def o_idx(step):
        bi, h, i = tail_unit(step)
        return bi, h, i, 0

    in_specs = [
        pl.BlockSpec(lam_vec.shape, lambda step: (0, 0)),
        pl.BlockSpec((1, DA_V_DIM), lambda step: (0, 0)),
        pl.BlockSpec((1, 1, tq, LANES), q_idx),
        pl.BlockSpec((1, 1, t, LANES), k_idx),
        pl.BlockSpec((1, 1, t, DA_V_DIM), v_idx),
    ]
    args = [lam_vec, sub_g.reshape(1, DA_V_DIM), q, k, v]
    if cache is not None:
        cache_k, cache_v, slot = cache

        def ck_idx(step):
            bi, h, _ = head_unit(step)
            return bi, slot, 0, h

        def cv_idx(step):
            bi, h, _ = tail_unit(step)
            return bi, slot, 0, h

        in_specs += [pl.BlockSpec((1, 1, past, LANES), ck_idx), pl.BlockSpec((1, 1, past, DA_V_DIM), cv_idx)]
        args += [cache_k, cache_v]
    return pl.pallas_call(
        kern,
        grid=(n_units + 1,),
        in_specs=in_specs,
        out_specs=pl.BlockSpec((1, 1, tq, DA_V_DIM), o_idx),
        out_shape=jax.ShapeDtypeStruct((b, nh, t, DA_V_DIM), BF16),
        scratch_shapes=[pltpu.VMEM((nkc, 2 * tq, kc), F32)] * 2 + [pltpu.VMEM((2 * tq, 1), F32)] * 2,
        compiler_params=_cparams("arbitrary"),
        name="diff_attention",
    )(*args)


def _ret_kernel(*refs, chunk, nc, has_s0, emit_state):
    dec_ref, q_ref, k_ref, v_ref = refs[:4]
    i = 4
    if has_s0:
        s0_ref = refs[i]
        i += 1
    o_ref = refs[i]
    i += 1
    if emit_state:
        st_ref = refs[i]
        i += 1
    sf_ref, sb_ref = refs[i:i + 2]
    c_ = chunk
    lg = jnp.log1p(-jnp.exp(dec_ref[0]))
    lgf, lgb = lg[0:1, :RET_DK], lg[1:2, :RET_DK]
    row = lax.broadcasted_iota(jnp.int32, (c_, RET_DK), 0).astype(F32)
    k_dec_f = jnp.exp((c_ - 1.0 - row) * lgf)
    k_dec_b = jnp.exp(row * lgb)
    q_dec_f = jnp.exp((row + 1.0) * lgf)
    q_dec_b = jnp.exp((c_ - row) * lgb)
    chunk_dec_f = jnp.exp(c_ * lg[0:1])
    chunk_dec_b = jnp.exp(c_ * lg[1:2])
    diff = row - lax.broadcasted_iota(jnp.int32, (c_, c_), 1).astype(F32)
    dmat = (jnp.where(diff >= 0, jnp.exp(jnp.where(diff >= 0, diff, 0.0) * lgf), 0.0)
            + jnp.where(diff <= 0, jnp.exp(jnp.where(diff <= 0, -diff, 0.0) * lgb), 0.0))

    def rows(c):
        return pl.ds(pl.multiple_of(c * c_, c_), c_)

    def chunk_kv(c, carry):
        kc = k_ref[0, 0, rows(c), :].astype(F32)
        vc = v_ref[0, 0, rows(c), :]
        kd = jnp.concatenate([(kc * k_dec_f).astype(BF16), (kc * k_dec_b).astype(BF16)], axis=1)
        u = _dot_tn(kd, vc)
        sf_ref[c] = u[:RET_DK]
        sb_ref[c] = u[RET_DK:]
        return carry

    lax.fori_loop(0, nc, chunk_kv, 0, unroll=min(nc, CHUNK_UNROLL))

    if has_s0:
        s_f, s_b = s0_ref[0, 0, 0], s0_ref[0, 1, 0]
    else:
        s_f = s_b = jnp.zeros((RET_DK, RET_DV), F32)

    def scan_f(c, s):
        u = sf_ref[c]
        sf_ref[c] = s
        return chunk_dec_f * s + u

    def scan_b(j, s):
        c = nc - 1 - j
        u = sb_ref[c]
        sb_ref[c] = s
        return chunk_dec_b * s + u

    s_f = lax.fori_loop(0, nc, scan_f, s_f)
    s_b = lax.fori_loop(0, nc, scan_b, s_b)
    if emit_state:
        st_ref[0, 0, 0] = s_f
        st_ref[0, 1, 0] = s_b

    def chunk_out(c, carry):
        qc = q_ref[0, 0, rows(c), :]
        kc = k_ref[0, 0, rows(c), :]
        vc = v_ref[0, 0, rows(c), :]
        sc = (_dot_nt(qc, kc) * dmat).astype(BF16)
        qf = qc.astype(F32)
        lhs = jnp.concatenate([sc, (qf * q_dec_f).astype(BF16), (qf * q_dec_b).astype(BF16)], axis=1)
        rhs = jnp.concatenate([vc, sf_ref[c].astype(BF16), sb_ref[c].astype(BF16)], axis=0)
        o = _dot(lhs, rhs)
        o_ref[0, 0, rows(c), :] = _rms(o).astype(BF16)
        return carry

    lax.fori_loop(0, nc, chunk_out, 0, unroll=min(nc, CHUNK_UNROLL))


def _retention(q, k, v, dec, s0, emit_state):
    b, nh, t, _ = q.shape
    c_ = RET_CHUNK
    nc = t // c_
    has_s0 = s0 is not None
    kern = functools.partial(_ret_kernel, chunk=c_, nc=nc, has_s0=has_s0, emit_state=emit_state)
    in_specs = [
        pl.BlockSpec((1, 2, RET_DV), lambda bi, h: (h, 0, 0)),
        pl.BlockSpec((1, 1, t, RET_DK), lambda bi, h: (bi, h, 0, 0)),
        pl.BlockSpec((1, 1, t, RET_DK), lambda bi, h: (bi, h, 0, 0)),
        pl.BlockSpec((1, 1, t, RET_DV), lambda bi, h: (bi, h, 0, 0)),
    ]
    args = [dec, q, k, v]
    st_spec = pl.BlockSpec((1, 2, 1, RET_DK, RET_DV), lambda bi, h: (bi, 0, h, 0, 0))
    if has_s0:
        in_specs.append(st_spec)
        args.append(s0)
    out_specs = [pl.BlockSpec((1, 1, t, RET_DV), lambda bi, h: (bi, h, 0, 0))]
    out_shape = [jax.ShapeDtypeStruct((b, nh, t, RET_DV), BF16)]
    if emit_state:
        out_specs.append(st_spec)
        out_shape.append(jax.ShapeDtypeStruct((b, 2, nh, RET_DK, RET_DV), F32))
    return pl.pallas_call(
        kern,
        grid=(b, nh),
        in_specs=in_specs,
        out_specs=out_specs,
        out_shape=out_shape,
        scratch_shapes=[pltpu.VMEM((nc, RET_DK, RET_DV), F32)] * 2,
        compiler_params=_cparams("parallel", "parallel"),
        name="retention",
    )(*args)


def _gla_kernel(*refs, chunk, nc, has_s0, emit_state):
    ng_ref, q_ref, kf_ref, kb_ref, gf_ref, gb_ref, v_ref = refs[:7]
    i = 7
    if has_s0:
        s0_ref = refs[i]
        i += 1
    o_ref = refs[i]
    i += 1
    if emit_state:
        st_ref = refs[i]
        i += 1
    sf_ref, sb_ref, bf_ref, bb_ref = refs[i:i + 4]
    c_ = chunk
    mid = c_ // 2
    r = lax.broadcasted_iota(jnp.int32, (c_, c_), 0)
    cc = lax.broadcasted_iota(jnp.int32, (c_, c_), 1)
    tri_f = (r >= cc).astype(BF16)
    tri_b = (r <= cc).astype(BF16)
    r2 = lax.broadcasted_iota(jnp.int32, (c_, 2 * c_), 0)
    c2 = lax.broadcasted_iota(jnp.int32, (c_, 2 * c_), 1)
    mask_f = (c2 < c_) & (r2 >= c2)
    mask_b = (c2 >= c_) & (r2 <= c2 - c_)

    def rows(c):
        return pl.ds(pl.multiple_of(c * c_, c_), c_)

    group = min(nc, 8)
    for g_ref, tri, b_ref in ((gf_ref, tri_f, bf_ref), (gb_ref, tri_b, bb_ref)):
        for c0 in range(0, nc, group):
            g = jnp.concatenate([g_ref[0, 0, (c0 + j) * c_:(c0 + j + 1) * c_, :] for j in range(group)], axis=1)
            hi = g.astype(BF16)
            lo = (g - hi.astype(F32)).astype(BF16)
            b = _dot(tri, hi) + _dot(tri, lo)
            for j in range(group):
                b_ref[(c0 + j) * c_:(c0 + j + 1) * c_, :] = b[:, j * HG_DK:(j + 1) * HG_DK]

    def chunk_kv(c, carry):
        bf = bf_ref[rows(c), :]
        bb = bb_ref[rows(c), :]
        kdf = (kf_ref[0, 0, rows(c), :].astype(F32) * jnp.exp(bf[c_ - 1:c_] - bf)).astype(BF16)
        kdb = (kb_ref[0, 0, rows(c), :].astype(F32) * jnp.exp(bb[0:1] - bb)).astype(BF16)
        u = _dot_tn(v_ref[0, 0, rows(c), :], jnp.concatenate([kdf, kdb], axis=1))
        sf_ref[c] = u[:, :HG_DK]
        sb_ref[c] = u[:, HG_DK:]
        return carry

    lax.fori_loop(0, nc, chunk_kv, 0, unroll=min(nc, CHUNK_UNROLL))

    if has_s0:
        s_f, s_b = s0_ref[0, 0, 0].T, s0_ref[0, 1, 0].T
    else:
        s_f = s_b = jnp.zeros((HG_DV, HG_DK), F32)

    def scan_f(c, s):
        u = sf_ref[c]
        sf_ref[c] = s
        last = pl.multiple_of(c * c_, c_) + (c_ - 1)
        return jnp.exp(bf_ref[pl.ds(last, 1), :]) * s + u

    def scan_b(j, s):
        c = nc - 1 - j
        u = sb_ref[c]
        sb_ref[c] = s
        first = pl.multiple_of(c * c_, c_)
        return jnp.exp(bb_ref[pl.ds(first, 1), :]) * s + u

    s_f = lax.fori_loop(0, nc, scan_f, s_f)
    s_b = lax.fori_loop(0, nc, scan_b, s_b)
    if emit_state:
        st_ref[0, 0, 0] = s_f.T
        st_ref[0, 1, 0] = s_b.T

    def chunk_out(c, carry):
        q = q_ref[0, 0, rows(c), :].astype(F32)
        kf = kf_ref[0, 0, rows(c), :].astype(F32)
        kb = kb_ref[0, 0, rows(c), :].astype(F32)
        v = v_ref[0, 0, rows(c), :]
        bf = bf_ref[rows(c), :]
        bb = bb_ref[rows(c), :]
        ref_f = bf[mid:mid + 1]
        ref_b = bb[c_ - 1 - mid:c_ - mid]
        qq = jnp.concatenate([(q * jnp.exp(bf - ref_f)).astype(BF16), (q * jnp.exp(bb - ref_b)).astype(BF16)], axis=0)
        kk = jnp.concatenate([(kf * jnp.exp(ref_f - bf)).astype(BF16), (kb * jnp.exp(ref_b - bb)).astype(BF16)], axis=0)
        pr = _dot_nt(qq, kk)
        sc = jnp.where(mask_f, pr[:c_], 0.0) + jnp.where(mask_b, pr[c_:], 0.0)
        qcat = jnp.concatenate([(q * jnp.exp(bf)).astype(BF16), (q * jnp.exp(bb)).astype(BF16)], axis=1)
        scat = jnp.concatenate([sf_ref[c], sb_ref[c]], axis=1).astype(BF16)
        o = _dot(sc.astype(BF16), jnp.concatenate([v, v], axis=0)) + _dot_nt(qcat, scat)
        o_ref[0, 0, rows(c), :] = (_rms(o) * ng_ref[...]).astype(BF16)
        return carry

    lax.fori_loop(0, nc, chunk_out, 0, unroll=min(nc, CHUNK_UNROLL))


def _gla(q, kf, kb, gf, gb, v, norm_g, s0, emit_state):
    b, nh, t, _ = q.shape
    c_ = GLA_CHUNK
    nc = t // c_
    has_s0 = s0 is not None
    kern = functools.partial(_gla_kernel, chunk=c_, nc=nc, has_s0=has_s0, emit_state=emit_state)
    seq = pl.BlockSpec((1, 1, t, HG_DK), lambda bi, h: (bi, h, 0, 0))
    in_specs = [pl.BlockSpec((1, HG_DV), lambda bi, h: (0, 0))] + [seq] * 6
    args = [norm_g.reshape(1, HG_DV), q, kf, kb, gf, gb, v]
    st_spec = pl.BlockSpec((1, 2, 1, HG_DK, HG_DV), lambda bi, h: (bi, 0, h, 0, 0))
    if has_s0:
        in_specs.append(st_spec)
        args.append(s0)
    out_specs = [pl.BlockSpec((1, 1, t, HG_DV), lambda bi, h: (bi, h, 0, 0))]
    out_shape = [jax.ShapeDtypeStruct((b, nh, t, HG_DV), BF16)]
    if emit_state:
        out_specs.append(st_spec)
        out_shape.append(jax.ShapeDtypeStruct((b, 2, nh, HG_DK, HG_DV), F32))
    return pl.pallas_call(
        kern,
        grid=(b, nh),
        in_specs=in_specs,
        out_specs=out_specs,
        out_shape=out_shape,
        scratch_shapes=[pltpu.VMEM((nc, HG_DV, HG_DK), F32)] * 2 + [pltpu.VMEM((t, HG_DK), F32)] * 2,
        compiler_params=_cparams("parallel", "parallel"),
        name="gla",
    )(*args)


def _out_kernel(o_ref, g_ref, x_ref, w_ref, mod_ref, gpost_ref, y_ref):
    d = x_ref.shape[-1]
    o = jnp.concatenate([o_ref[0, h] for h in range(o_ref.shape[1])], axis=-1).astype(F32)
    z = (o * _silu(g_ref[0].astype(F32))).astype(BF16)
    y = _rms(_dot(z, w_ref[...])) * gpost_ref[...]
    y_ref[0] = x_ref[0] + mod_ref[0, :, 2 * d:3 * d] * y


def _out_call(o, g, x, w, mod, gpost, tm):
    b, t, d = x.shape
    nh, dv = o.shape[1], o.shape[3]
    wd = nh * dv
    bm = mod.shape[0]
    mod_idx = (lambda bi, i: (bi, 0, 0)) if bm > 1 else (lambda bi, i: (0, 0, 0))
    return pl.pallas_call(
        _out_kernel,
        grid=(b, t // tm),
        in_specs=[
            pl.BlockSpec((1, nh, tm, dv), lambda bi, i: (bi, 0, i, 0)),
            pl.BlockSpec((1, tm, wd), lambda bi, i: (bi, i, 0)),
            pl.BlockSpec((1, tm, d), lambda bi, i: (bi, i, 0)),
            pl.BlockSpec((wd, d), lambda bi, i: (0, 0)),
            pl.BlockSpec((1, 1, mod.shape[2]), mod_idx),
            pl.BlockSpec((1, d), lambda bi, i: (0, 0)),
        ],
        out_specs=pl.BlockSpec((1, tm, d), lambda bi, i: (bi, i, 0)),
        out_shape=jax.ShapeDtypeStruct((b, t, d), F32),
        compiler_params=_cparams("parallel", "parallel"),
        name="out_proj",
    )(o, g, x, w, mod, gpost)


def _rope_tables(t):
    rows = t // GRID_W
    row = jnp.repeat(jnp.arange(rows), GRID_W)
    col = jnp.broadcast_to(jnp.arange(GRID_W), (rows, GRID_W)).reshape(-1)
    n_pair = DA_HEAD_DIM // 4
    inv = ROPE_BASE ** (-jnp.arange(n_pair, dtype=F32) / n_pair)
    ang = jnp.concatenate([row[:, None] * inv, col[:, None] * inv], axis=-1)
    cos = jnp.repeat(jnp.cos(ang), 2, axis=-1)
    sin = jnp.repeat(jnp.sin(ang), 2, axis=-1)
    odd = (jnp.arange(DA_HEAD_DIM) % 2) == 1
    sin_prev = jnp.where(odd, sin, 0.0)
    sin_next = jnp.where(odd, 0.0, -sin)
    tile = lambda a: jnp.tile(a, (1, LANES // DA_HEAD_DIM))
    return tile(cos), tile(sin_prev), tile(sin_next)


def kernel(x_prompt, x_sample, cache_k, cache_v, state_ret, state_hgrn, c, c_ctx,
           w_mod, b_mod, g_pre, g_post,
           da_w_in, da_w_out, da_lambda, da_subln,
           ret_w_in, ret_w_out, ret_decay,
           hg_w_in, hg_w_out, hg_lb, hg_norm):
    depth, d, _ = w_mod.shape
    db = x_sample.shape[0]
    ts = x_sample.shape[1]
    tp = x_prompt.shape[1]
    n_mix = 3
    kinds = [l % n_mix for l in range(depth)]
    slots = [kinds[:l].count(kinds[l]) for l in range(depth)]

    cvec = jnp.zeros((16, d), F32).at[:db].set(c).at[db].set(c_ctx)
    mods = _modulation_all(cvec, w_mod, b_mod)

    rope = _rope_tables(ts)
    tab_spec = pl.BlockSpec((256, LANES), lambda bi, i: (i, 0))
    tm = 256

    n_diff = kinds.count(0)
    past = cache_k.shape[2]
    cache_k = cache_k.reshape(db, n_diff, past, -1)
    cache_v = cache_v.reshape(db, n_diff, past, -1)
    xp, xs = x_prompt, x_sample
    new_k = new_v = None
    new_r, new_h = [], []
    for l in range(depth):
        kind, j = kinds[l], slots[l]
        mod_p = mods[l, db:db + 1].reshape(1, 1, 3 * d)
        mod_s = mods[l, :db].reshape(db, 1, 3 * d)
        gpre = g_pre[l].reshape(1, d)
        gpost = g_post[l].reshape(1, d)
        if kind == 0:
            w_in = da_w_in[j].astype(BF16)
            w_out = da_w_out[j].astype(BF16)
            heads = [("heads", LANES, BF16)] * 3 + [("plain", d, BF16)]
            carried = [] if new_k is None else [new_k, new_v]
            cache_outs = [("slot", d, F32, n_diff, j, new_k), ("slot", d, F32, n_diff, j, new_v)]
            qp, kp, vp, gp, new_k, new_v = _proj_call(
                functools.partial(_da_proj_kernel, rope=False, emit_cache=True,
                                  cache_slot=0 if carried else j, n_carried=len(carried)),
                xp, mod_p, gpre, w_in, [], [], heads + cache_outs, "da_proj_prompt", tm)
            qs, ks, vs, gs = _proj_call(
                functools.partial(_da_proj_kernel, rope=True, emit_cache=False),
                xs, mod_s, gpre, w_in, list(rope), [tab_spec] * 3, heads, "da_proj_sample", tm)
            op = _attention(qp, kp, vp, da_lambda[j], da_subln[j], l, tq=256)
            os_ = _attention(qs, ks, vs, da_lambda[j], da_subln[j], l, tq=min(512, ts), cache=(cache_k, cache_v, j))
        elif kind == 1:
            w_in = ret_w_in[j].astype(BF16)
            w_out = ret_w_out[j].astype(BF16)
            outs = [("heads", RET_DK, BF16)] * 2 + [("heads", RET_DV, BF16), ("plain", N_HEADS * RET_DV, BF16)]
            qp, kp, vp, gp = _proj_call(_ret_proj_kernel, xp, mod_p, gpre, w_in, [], [], outs,
                                        "ret_proj_prompt", tm)
            qs, ks, vs, gs = _proj_call(_ret_proj_kernel, xs, mod_s, gpre, w_in, [], [], outs,
                                        "ret_proj_sample", tm)
            dec = jnp.broadcast_to(ret_decay[j].T[:, :, None], (N_HEADS, 2, RET_DV))
            op, st = _retention(qp, kp, vp, dec, None, True)
            (os_,) = _retention(qs, ks, vs, dec, state_ret[:, j], False)
            new_r.append(st)
        else:
            w_in = hg_w_in[j].astype(BF16)
            w_out = hg_w_out[j].astype(BF16)
            lb = hg_lb.reshape(2 * depth, -1)
            lb_spec = pl.BlockSpec(lb.shape, lambda bi, i: (0, 0))
            outs = ([("heads", HG_DK, BF16)] * 3 + [("heads", HG_DK, F32)] * 2
                    + [("heads", HG_DV, BF16), ("plain", d, BF16)])
            kern = functools.partial(_hg_proj_kernel, layer_idx=l, depth=depth)
            pp = _proj_call(kern, xp, mod_p, gpre, w_in, [lb], [lb_spec], outs, "hg_proj_prompt", tm)
            ps = _proj_call(kern, xs, mod_s, gpre, w_in, [lb], [lb_spec], outs, "hg_proj_sample", tm)
            gp, gs = pp[6], ps[6]
            op, st = _gla(*pp[:6], hg_norm[j], None, True)
            (os_,) = _gla(*ps[:6], hg_norm[j], state_hgrn[:, j], False)
            new_h.append(st)
        xp = _out_call(op, gp, xp, w_out, mod_p, gpost, tm)
        xs = _out_call(os_, gs, xs, w_out, mod_s, gpost, tm)

    bp = x_prompt.shape[0]
    new_cache_k = new_k.reshape(bp, n_diff, tp, 2 * N_HEADS, DA_HEAD_DIM)
    new_cache_v = new_v.reshape(bp, n_diff, tp, N_HEADS, DA_V_DIM)
    stack = lambda parts: parts[0][:, None] if len(parts) == 1 else jnp.stack(parts, axis=1)
    new_state_ret = stack(new_r)
    new_state_hgrn = stack(new_h)
    return (xp, xs, new_cache_k, new_cache_v, new_state_ret, new_state_hgrn)
```

```python
import functools
import math

import jax
import jax.numpy as jnp
from jax import lax
from jax.experimental import pallas as pl
from jax.experimental.pallas import tpu as pltpu

F32 = jnp.float32
BF16 = jnp.bfloat16

EPS = 1e-6
LANES = 128
GRID_W = 64
ROPE_BASE = 10000.0
N_HEADS = 8
DA_HEAD_DIM = 64
DA_V_DIM = 128
RET_DK = 128
RET_DV = 256
HG_DK = 128
HG_DV = 128
RET_CHUNK = 256
GLA_CHUNK = 64
CHUNK_UNROLL = 16
VMEM_LIMIT = 56 * 1024 * 1024
Q_SCALE_LOG2 = DA_HEAD_DIM ** -0.5 * math.log2(math.e)


def _cparams(*sem):
    return pltpu.CompilerParams(dimension_semantics=sem, vmem_limit_bytes=VMEM_LIMIT)


def _dot(a, b):
    return jnp.dot(a, b, preferred_element_type=F32)


def _dot_nt(a, b):
    return lax.dot_general(a, b, (((1,), (1,)), ((), ())), preferred_element_type=F32)


def _dot_tn(a, b):
    return lax.dot_general(a, b, (((0,), (0,)), ((), ())), preferred_element_type=F32)


def _silu(x):
    return x * jax.nn.sigmoid(x)


def _rms(x):
    return x * lax.rsqrt(jnp.mean(x * x, axis=-1, keepdims=True) + EPS)


def _mod_kernel(c_ref, w_ref, b_ref, o_ref):
    s = _silu(c_ref[...]).astype(BF16)
    o_ref[0] = _dot(s, w_ref[0].astype(BF16)) + b_ref[0]


def _modulation_all(cvec, w_mod, b_mod):
    depth, d, n = w_mod.shape
    r = cvec.shape[0]
    tn = 1024
    return pl.pallas_call(
        _mod_kernel,
        grid=(depth, n // tn),
        in_specs=[
            pl.BlockSpec((r, d), lambda l, j: (0, 0)),
            pl.BlockSpec((1, d, tn), lambda l, j: (l, 0, j)),
            pl.BlockSpec((1, 1, tn), lambda l, j: (l, 0, j)),
        ],
        out_specs=pl.BlockSpec((1, r, tn), lambda l, j: (l, 0, j)),
        out_shape=jax.ShapeDtypeStruct((depth, r, n), F32),
        compiler_params=_cparams("parallel", "parallel"),
        name="modulation",
    )(cvec, w_mod, b_mod.reshape(depth, 1, n))


PROJ_NC = 512


def _norm_mod(x_ref, gpre_ref, mod_ref):
    d = x_ref.shape[-1]
    xn = _rms(x_ref[0]) * gpre_ref[...]
    shift = mod_ref[0, :, 0:d]
    scale = mod_ref[0, :, d:2 * d]
    return (xn * (1.0 + scale) + shift).astype(BF16)


def _store_heads(ref, acc, col0, dh, dtype):
    for j in range(acc.shape[1] // dh):
        ref[0, col0 // dh + j] = acc[:, j * dh:(j + 1) * dh].astype(dtype)


def _da_proj_kernel(*refs, rope, emit_cache, cache_slot=0, n_carried=0):
    x_ref, gpre_ref, mod_ref, w_ref = refs[:4]
    i = 4
    if rope:
        cos_ref, sinp_ref, sinn_ref = refs[i:i + 3]
        i += 3
    i += n_carried
    q_ref, k_ref, v_ref, g_ref = refs[i:i + 4]
    i += 4
    d = x_ref.shape[-1]
    if emit_cache:
        kc_ref, vc_ref = refs[i:i + 2]
        for ref in (kc_ref, vc_ref):
            for s in range(ref.shape[1]):
                if s != cache_slot:
                    ref[0, s] = jnp.zeros(ref.shape[2:], ref.dtype)
    h = _norm_mod(x_ref, gpre_ref, mod_ref)

    def rot(a):
        out = []
        for j in range(a.shape[1] // LANES):
            t = a[:, j * LANES:(j + 1) * LANES]
            out.append(t * cos_ref[...] + pltpu.roll(t, 1, 1) * sinp_ref[...]
                       + pltpu.roll(t, LANES - 1, 1) * sinn_ref[...])
        return jnp.concatenate(out, axis=1)

    for sec in range(4):
        for c0 in range(0, d, PROJ_NC):
            acc = _dot(h, w_ref[:, sec * d + c0: sec * d + c0 + PROJ_NC])
            if sec == 0:
                if rope:
                    acc = rot(acc)
                _store_heads(q_ref, acc * Q_SCALE_LOG2, c0, LANES, BF16)
            elif sec == 1:
                if emit_cache:
                    kc_ref[0, cache_slot, :, c0:c0 + PROJ_NC] = acc
                if rope:
                    acc = rot(acc)
                _store_heads(k_ref, acc, c0, LANES, BF16)
            elif sec == 2:
                if emit_cache:
                    vc_ref[0, cache_slot, :, c0:c0 + PROJ_NC] = acc
                _store_heads(v_ref, acc, c0, DA_V_DIM, BF16)
            else:
                g_ref[0, :, c0:c0 + PROJ_NC] = _silu(acc).astype(BF16)


def _ret_proj_kernel(x_ref, gpre_ref, mod_ref, w_ref, q_ref, k_ref, v_ref, g_ref):
    d = x_ref.shape[-1]
    h = _norm_mod(x_ref, gpre_ref, mod_ref)
    qk = N_HEADS * RET_DK
    vw = N_HEADS * RET_DV
    for c0 in range(0, qk, PROJ_NC):
        _store_heads(q_ref, _dot(h, w_ref[:, c0:c0 + PROJ_NC]), c0, RET_DK, BF16)
    for c0 in range(0, qk, PROJ_NC):
        acc = _dot(h, w_ref[:, qk + c0: qk + c0 + PROJ_NC]) * (RET_DK ** -0.5)
        _store_heads(k_ref, acc, c0, RET_DK, BF16)
    for c0 in range(0, vw, PROJ_NC):
        _store_heads(v_ref, _dot(h, w_ref[:, 2 * qk + c0: 2 * qk + c0 + PROJ_NC]), c0, RET_DV, BF16)
    for c0 in range(0, vw, PROJ_NC):
        g_ref[0, :, c0:c0 + PROJ_NC] = _silu(_dot(
            h, w_ref[:, 2 * qk + vw + c0: 2 * qk + vw + c0 + PROJ_NC])).astype(BF16)


def _hg_proj_kernel(x_ref, gpre_ref, mod_ref, w_ref, lb_ref,
                    q_ref, kf_ref, kb_ref, gf_ref, gb_ref, v_ref, g_ref, *, layer_idx, depth):
    d = x_ref.shape[-1]
    h = _norm_mod(x_ref, gpre_ref, mod_ref)
    f_w = N_HEADS * HG_DK

    def lower_bound(direction):
        lg = lb_ref[direction * depth:(direction + 1) * depth, :]
        e = jnp.exp(lg - jnp.max(lg, axis=0, keepdims=True))
        sm = e / jnp.sum(e, axis=0, keepdims=True)
        cum = sm[0:1]
        for i in range(1, layer_idx + 1):
            cum = cum + sm[i:i + 1]
        return cum - sm[0:1]

    lbs = (lower_bound(0), lower_bound(1))
    for c0 in range(0, f_w, PROJ_NC):
        _store_heads(q_ref, _silu(_dot(h, w_ref[:, c0:c0 + PROJ_NC])), c0, HG_DK, BF16)
    for direction, (k_out, g_out) in enumerate(((kf_ref, gf_ref), (kb_ref, gb_ref))):
        base = (1 + direction) * f_w
        for c0 in range(0, f_w, PROJ_NC):
            acc = _dot(h, w_ref[:, base + c0: base + c0 + PROJ_NC])
            lbd = lbs[direction][:, c0:c0 + PROJ_NC]
            f = lbd + (1.0 - lbd) * jax.nn.sigmoid(acc)
            _store_heads(g_out, jnp.log(f), c0, HG_DK, F32)
            _store_heads(k_out, 1.0 - f, c0, HG_DK, BF16)
    for c0 in range(0, d, PROJ_NC):
        _store_heads(v_ref, _dot(h, w_ref[:, 3 * f_w + c0: 3 * f_w + c0 + PROJ_NC]), c0, HG_DV, BF16)
    for c0 in range(0, d, PROJ_NC):
        g_ref[0, :, c0:c0 + PROJ_NC] = _silu(_dot(
            h, w_ref[:, 3 * f_w + d + c0: 3 * f_w + d + c0 + PROJ_NC])).astype(BF16)


def _proj_call(kernel, x, mod, gpre, w, extra_in, extra_specs, outs, name, tm):
    b, t, d = x.shape
    n = w.shape[1]
    bm = mod.shape[0]
    mod_idx = (lambda bi, i: (bi, 0, 0)) if bm > 1 else (lambda bi, i: (0, 0, 0))
    in_specs = [
        pl.BlockSpec((1, tm, d), lambda bi, i: (bi, i, 0)),
        pl.BlockSpec((1, d), lambda bi, i: (0, 0)),
        pl.BlockSpec((1, 1, mod.shape[2]), mod_idx),
        pl.BlockSpec((d, n), lambda bi, i: (0, 0)),
    ] + extra_specs
    args = [x, gpre, mod, w, *extra_in]
    out_specs, out_shape, aliases = [], [], {}
    for out in outs:
        kind, width, dtype = out[:3]
        if kind == "heads":
            out_specs.append(pl.BlockSpec((1, N_HEADS, tm, width), lambda bi, i: (bi, 0, i, 0)))
            out_shape.append(jax.ShapeDtypeStruct((b, N_HEADS, t, width), dtype))
        elif kind == "plain":
            out_specs.append(pl.BlockSpec((1, tm, width), lambda bi, i: (bi, i, 0)))
            out_shape.append(jax.ShapeDtypeStruct((b, t, width), dtype))
        else:
            n_slots, slot, carried = out[3:]
            out_shape.append(jax.ShapeDtypeStruct((b, n_slots, t, width), dtype))
            if carried is None:
                out_specs.append(pl.BlockSpec((1, n_slots, tm, width), lambda bi, i: (bi, 0, i, 0)))
            else:
                out_specs.append(pl.BlockSpec((1, 1, tm, width), functools.partial(lambda bi, i, s: (bi, s, i, 0), s=slot)))
                aliases[len(args)] = len(out_shape) - 1
                in_specs.append(pl.BlockSpec(memory_space=pl.ANY))
                args.append(carried)
    return pl.pallas_call(
        kernel,
        grid=(b, t // tm),
        in_specs=in_specs,
        out_specs=out_specs,
        out_shape=out_shape,
        input_output_aliases=aliases,
        compiler_params=_cparams("parallel", "parallel"),
        name=name,
    )(*args)


def _attn_stages(lam_ref, subg_ref, q_ref, k_ref, v_ref, ck_ref, cv_ref, o_ref, s_w, m_w, s_r, m_r,
                 *, tq, kc, nkc, nkc_own, hpu, lam_init):
    def chunk_of(own_ref, cache_ref, hh, c):
        if c < nkc_own:
            return own_ref[0, hh, c * kc:(c + 1) * kc, :]
        rows = slice((c - nkc_own) * kc, (c - nkc_own + 1) * kc)
        return cache_ref[0, 0, rows, hh * LANES:(hh + 1) * LANES].astype(BF16)

    lane = lax.broadcasted_iota(jnp.int32, (tq, LANES), 1)
    lf = lam_ref[...]
    lam = (jnp.exp(jnp.sum(lf[0:1] * lf[1:2], axis=-1, keepdims=True))
           - jnp.exp(jnp.sum(lf[2:3] * lf[3:4], axis=-1, keepdims=True)) + lam_init)
    for hh in range(hpu):
        q = q_ref[0, hh]
        zero = jnp.zeros_like(q)
        qs = jnp.concatenate([jnp.where(lane < DA_HEAD_DIM, q, zero),
                              jnp.where(lane >= DA_HEAD_DIM, q, zero)], axis=0)
        m = jnp.full((2 * tq, LANES), -jnp.inf, F32)
        for c in range(nkc):
            s = _dot_nt(qs, chunk_of(k_ref, ck_ref, hh, c))
            s_w[hh * nkc + c] = s
            for j in range(kc // LANES):
                m = jnp.maximum(m, s[:, j * LANES:(j + 1) * LANES])
        m_w[hh] = jnp.broadcast_to(jnp.max(m, axis=-1, keepdims=True), m.shape)

        m_row = m_r[hh]
        acc = jnp.zeros((2 * tq, 2 * DA_V_DIM), F32)
        for c in range(nkc):
            p = jnp.exp2(s_r[hh * nkc + c] - jnp.tile(m_row, (1, kc // LANES))).astype(BF16)
            v = chunk_of(v_ref, cv_ref, hh, c)
            acc = acc + _dot(p, jnp.concatenate([v, jnp.ones_like(v)], axis=1))
        on = acc[:, :DA_V_DIM] / acc[:, DA_V_DIM:]
        o = on[:tq] - lam * on[tq:]
        o = _rms(o) * subg_ref[...] * (1.0 - lam_init)
        o_ref[0, hh] = o.astype(BF16)


def _attn_kernel(*refs, has_cache, **static):
    if has_cache:
        lam_ref, subg_ref, q_ref, k_ref, v_ref, ck_ref, cv_ref, o_ref, s0, s1, m0, m1 = refs
    else:
        lam_ref, subg_ref, q_ref, k_ref, v_ref, o_ref, s0, s1, m0, m1 = refs
        ck_ref = cv_ref = None
    t = pl.program_id(0)

    @pl.when(t == 0)
    def _init():
        s1[...] = jnp.zeros(s1.shape, F32)
        m1[...] = jnp.zeros(m1.shape, F32)

    io = (lam_ref, subg_ref, q_ref, k_ref, v_ref, ck_ref, cv_ref, o_ref)

    @pl.when(t % 2 == 0)
    def _even():
        _attn_stages(*io, s0, m0, s1, m1, **static)

    @pl.when(t % 2 == 1)
    def _odd():
        _attn_stages(*io, s1, m1, s0, m0, **static)


def _attention(q, k, v, lam_vec, sub_g, layer_idx, tq, hpu=1, cache=None):
    b, nh, t, _ = q.shape
    nh = nh // hpu
    past = 0 if cache is None else cache[0].shape[2]
    kc = next(w for w in (512, 256, 128) if t % w == 0 and past % w == 0)
    nkc_own = t // kc
    nkc = nkc_own + past // kc
    nq = t // tq
    n_units = b * nh * nq
    lam_init = 0.8 - 0.6 * math.exp(-0.3 * layer_idx)
    kern = functools.partial(_attn_kernel, has_cache=cache is not None, tq=tq, kc=kc, nkc=nkc,
                             nkc_own=nkc_own, hpu=hpu, lam_init=lam_init)

    def unit(u):
        return u // (nh * nq), (u // nq) % nh, u % nq

    def head_unit(step):
        return unit(jnp.minimum(step, n_units - 1))

    def tail_unit(step):
        return unit(jnp.maximum(step - 1, 0))

    def q_idx(step):
        bi, h, i = head_unit(step)
        return bi, h, i, 0

    def k_idx(step):
        bi, h, _ = head_unit(step)
        return bi, h, 0, 0

    def v_idx(step):
        bi, h, _ = tail_unit(step)
        return bi, h, 0, 0

    def o_idx(step):
        bi, h, i = tail_unit(step)
        return bi, h, i, 0

    in_specs = [
        pl.BlockSpec(lam_vec.shape, lambda step: (0, 0)),
        pl.BlockSpec((1, DA_V_DIM), lambda step: (0, 0)),
        pl.BlockSpec((1, hpu, tq, LANES), q_idx),
        pl.BlockSpec((1, hpu, t, LANES), k_idx),
        pl.BlockSpec((1, hpu, t, DA_V_DIM), v_idx),
    ]
    args = [lam_vec, sub_g.reshape(1, DA_V_DIM), q, k, v]
    if cache is not None:
        cache_k, cache_v, slot = cache

        def ck_idx(step):
            bi, h, _ = head_unit(step)
            return bi, slot, 0, h

        def cv_idx(step):
            bi, h, _ = tail_unit(step)
            return bi, slot, 0, h

        in_specs += [pl.BlockSpec((1, 1, past, hpu * LANES), ck_idx),
                     pl.BlockSpec((1, 1, past, hpu * DA_V_DIM), cv_idx)]
        args += [cache_k, cache_v]
    return pl.pallas_call(
        kern,
        grid=(n_units + 1,),
        in_specs=in_specs,
        out_specs=pl.BlockSpec((1, hpu, tq, DA_V_DIM), o_idx),
        out_shape=jax.ShapeDtypeStruct(q.shape, BF16),
        scratch_shapes=[pltpu.VMEM((hpu * nkc, 2 * tq, kc), F32)] * 2 + [pltpu.VMEM((hpu, 2 * tq, LANES), F32)] * 2,
        compiler_params=_cparams("arbitrary"),
        name="diff_attention",
    )(*args)


def _ret_kernel(*refs, chunk, nc, has_s0, emit_state):
    dec_ref, q_ref, k_ref, v_ref = refs[:4]
    i = 4
    if has_s0:
        s0_ref = refs[i]
        i += 1
    o_ref = refs[i]
    i += 1
    if emit_state:
        st_ref = refs[i]
        i += 1
    sf_ref, sb_ref = refs[i:i + 2]
    c_ = chunk
    lg = jnp.log1p(-jnp.exp(dec_ref[0]))
    lgf, lgb = lg[0:1, :RET_DK], lg[1:2, :RET_DK]
    row = lax.broadcasted_iota(jnp.int32, (c_, RET_DK), 0).astype(F32)
    k_dec_f = jnp.exp((c_ - 1.0 - row) * lgf)
    k_dec_b = jnp.exp(row * lgb)
    q_dec_f = jnp.exp((row + 1.0) * lgf)
    q_dec_b = jnp.exp((c_ - row) * lgb)
    chunk_dec_f = jnp.exp(c_ * lg[0:1])
    chunk_dec_b = jnp.exp(c_ * lg[1:2])
    diff = (lax.broadcasted_iota(jnp.int32, (c_, c_), 0) - lax.broadcasted_iota(jnp.int32, (c_, c_), 1)).astype(F32)
    dmat = (jnp.where(diff >= 0, jnp.exp(jnp.where(diff >= 0, diff, 0.0) * lg[0:1, :c_]), 0.0)
            + jnp.where(diff <= 0, jnp.exp(jnp.where(diff <= 0, -diff, 0.0) * lg[1:2, :c_]), 0.0))

    def rows(c):
        return pl.ds(pl.multiple_of(c * c_, c_), c_)

    def chunk_kv(c, carry):
        kc = k_ref[0, 0, rows(c), :].astype(F32)
        vc = v_ref[0, 0, rows(c), :]
        kd = jnp.concatenate([(kc * k_dec_f).astype(BF16), (kc * k_dec_b).astype(BF16)], axis=1)
        u = _dot_tn(kd, vc)
        sf_ref[c] = u[:RET_DK]
        sb_ref[c] = u[RET_DK:]
        return carry

    lax.fori_loop(0, nc, chunk_kv, 0, unroll=min(nc, CHUNK_UNROLL))

    if has_s0:
        s_f, s_b = s0_ref[0, 0, 0], s0_ref[0, 1, 0]
    else:
        s_f = s_b = jnp.zeros((RET_DK, RET_DV), F32)

    def scan(j, carry):
        s_f, s_b = carry
        cb = nc - 1 - j
        u_f, u_b = sf_ref[j], sb_ref[cb]
        sf_ref[j] = s_f
        sb_ref[cb] = s_b
        return chunk_dec_f * s_f + u_f, chunk_dec_b * s_b + u_b

    s_f, s_b = lax.fori_loop(0, nc, scan, (s_f, s_b))
    if emit_state:
        st_ref[0, 0, 0] = s_f
        st_ref[0, 1, 0] = s_b

    def chunk_out(c, carry):
        qc = q_ref[0, 0, rows(c), :]
        kc = k_ref[0, 0, rows(c), :]
        vc = v_ref[0, 0, rows(c), :]
        sc = (_dot_nt(qc, kc) * dmat).astype(BF16)
        qf = qc.astype(F32)
        lhs = jnp.concatenate([sc, (qf * q_dec_f).astype(BF16), (qf * q_dec_b).astype(BF16)], axis=1)
        rhs = jnp.concatenate([vc, sf_ref[c].astype(BF16), sb_ref[c].astype(BF16)], axis=0)
        o = _dot(lhs, rhs)
        o_ref[0, 0, rows(c), :] = _rms(o).astype(BF16)
        return carry

    lax.fori_loop(0, nc, chunk_out, 0, unroll=min(nc, CHUNK_UNROLL))


def _retention(q, k, v, dec, s0, emit_state):
    b, nh, t, _ = q.shape
    c_ = RET_CHUNK
    nc = t // c_
    has_s0 = s0 is not None
    kern = functools.partial(_ret_kernel, chunk=c_, nc=nc, has_s0=has_s0, emit_state=emit_state)
    in_specs = [
        pl.BlockSpec((1, 2, RET_DV), lambda bi, h: (h, 0, 0)),
        pl.BlockSpec((1, 1, t, RET_DK), lambda bi, h: (bi, h, 0, 0)),
        pl.BlockSpec((1, 1, t, RET_DK), lambda bi, h: (bi, h, 0, 0)),
        pl.BlockSpec((1, 1, t, RET_DV), lambda bi, h: (bi, h, 0, 0)),
    ]
    args = [dec, q, k, v]
    st_spec = pl.BlockSpec((1, 2, 1, RET_DK, RET_DV), lambda bi, h: (bi, 0, h, 0, 0))
    if has_s0:
        in_specs.append(st_spec)
        args.append(s0)
    out_specs = [pl.BlockSpec((1, 1, t, RET_DV), lambda bi, h: (bi, h, 0, 0))]
    out_shape = [jax.ShapeDtypeStruct((b, nh, t, RET_DV), BF16)]
    if emit_state:
        out_specs.append(st_spec)
        out_shape.append(jax.ShapeDtypeStruct((b, 2, nh, RET_DK, RET_DV), F32))
    return pl.pallas_call(
        kern,
        grid=(b, nh),
        in_specs=in_specs,
        out_specs=out_specs,
        out_shape=out_shape,
        scratch_shapes=[pltpu.VMEM((nc, RET_DK, RET_DV), F32)] * 2,
        compiler_params=_cparams("parallel", "parallel"),
        name="retention",
    )(*args)


def _gla_kernel(*refs, chunk, nc, has_s0, emit_state):
    ng_ref, q_ref, kf_ref, kb_ref, gf_ref, gb_ref, v_ref = refs[:7]
    i = 7
    if has_s0:
        s0_ref = refs[i]
        i += 1
    o_ref = refs[i]
    i += 1
    if emit_state:
        st_ref = refs[i]
        i += 1
    sf_ref, sb_ref, bf_ref, bb_ref = refs[i:i + 4]
    c_ = chunk
    mid = c_ // 2
    r = lax.broadcasted_iota(jnp.int32, (c_, c_), 0)
    cc = lax.broadcasted_iota(jnp.int32, (c_, c_), 1)
    tri_f = (r >= cc).astype(BF16)
    tri_b = (r <= cc).astype(BF16)
    r2 = lax.broadcasted_iota(jnp.int32, (c_, 2 * c_), 0)
    c2 = lax.broadcasted_iota(jnp.int32, (c_, 2 * c_), 1)
    mask_f = (c2 < c_) & (r2 >= c2)
    mask_b = (c2 >= c_) & (r2 <= c2 - c_)

    def rows(c):
        return pl.ds(pl.multiple_of(c * c_, c_), c_)

    group = min(nc, 8)
    for g_ref, tri, b_ref in ((gf_ref, tri_f, bf_ref), (gb_ref, tri_b, bb_ref)):
        for c0 in range(0, nc, group):
            g = jnp.concatenate([g_ref[0, 0, (c0 + j) * c_:(c0 + j + 1) * c_, :] for j in range(group)], axis=1)
            hi = g.astype(BF16)
            lo = (g - hi.astype(F32)).astype(BF16)
            b = _dot(tri, hi) + _dot(tri, lo)
            for j in range(group):
                b_ref[(c0 + j) * c_:(c0 + j + 1) * c_, :] = b[:, j * HG_DK:(j + 1) * HG_DK]

    def chunk_kv(c, carry):
        bf = bf_ref[rows(c), :]
        bb = bb_ref[rows(c), :]
        kdf = (kf_ref[0, 0, rows(c), :].astype(F32) * jnp.exp(bf[c_ - 1:c_] - bf)).astype(BF16)
        kdb = (kb_ref[0, 0, rows(c), :].astype(F32) * jnp.exp(bb[0:1] - bb)).astype(BF16)
        u = _dot_tn(v_ref[0, 0, rows(c), :], jnp.concatenate([kdf, kdb], axis=1))
        sf_ref[c] = u[:, :HG_DK]
        sb_ref[c] = u[:, HG_DK:]
        return carry

    lax.fori_loop(0, nc, chunk_kv, 0, unroll=min(nc, CHUNK_UNROLL))

    if has_s0:
        s_f, s_b = s0_ref[0, 0, 0].T, s0_ref[0, 1, 0].T
    else:
        s_f = s_b = jnp.zeros((HG_DV, HG_DK), F32)

    def scan(j, carry):
        s_f, s_b = carry
        cb = nc - 1 - j
        u_f, u_b = sf_ref[j], sb_ref[cb]
        sf_ref[j] = s_f
        sb_ref[cb] = s_b
        last = pl.multiple_of(j * c_, c_) + (c_ - 1)
        first = pl.multiple_of(cb * c_, c_)
        return (jnp.exp(bf_ref[pl.ds(last, 1), :]) * s_f + u_f,
                jnp.exp(bb_ref[pl.ds(first, 1), :]) * s_b + u_b)

    s_f, s_b = lax.fori_loop(0, nc, scan, (s_f, s_b))
    if emit_state:
        st_ref[0, 0, 0] = s_f.T
        st_ref[0, 1, 0] = s_b.T

    def chunk_out(c, carry):
        q = q_ref[0, 0, rows(c), :].astype(F32)
        kf = kf_ref[0, 0, rows(c), :].astype(F32)
        kb = kb_ref[0, 0, rows(c), :].astype(F32)
        v = v_ref[0, 0, rows(c), :]
        bf = bf_ref[rows(c), :]
        bb = bb_ref[rows(c), :]
        ref_f = bf[mid:mid + 1]
        ref_b = bb[c_ - 1 - mid:c_ - mid]
        qq = jnp.concatenate([(q * jnp.exp(bf - ref_f)).astype(BF16), (q * jnp.exp(bb - ref_b)).astype(BF16)], axis=0)
        kk = jnp.concatenate([(kf * jnp.exp(ref_f - bf)).astype(BF16), (kb * jnp.exp(ref_b - bb)).astype(BF16)], axis=0)
        pr = _dot_nt(qq, kk)
        sc = jnp.where(mask_f, pr[:c_], 0.0) + jnp.where(mask_b, pr[c_:], 0.0)
        qcat = jnp.concatenate([(q * jnp.exp(bf)).astype(BF16), (q * jnp.exp(bb)).astype(BF16)], axis=1)
        scat = jnp.concatenate([sf_ref[c], sb_ref[c]], axis=1).astype(BF16)
        o = _dot(sc.astype(BF16), jnp.concatenate([v, v], axis=0)) + _dot_nt(qcat, scat)
        o_ref[0, 0, rows(c), :] = (_rms(o) * ng_ref[...]).astype(BF16)
        return carry

    lax.fori_loop(0, nc, chunk_out, 0, unroll=min(nc, CHUNK_UNROLL))


def _gla(q, kf, kb, gf, gb, v, norm_g, s0, emit_state):
    b, nh, t, _ = q.shape
    c_ = GLA_CHUNK
    nc = t // c_
    has_s0 = s0 is not None
    kern = functools.partial(_gla_kernel, chunk=c_, nc=nc, has_s0=has_s0, emit_state=emit_state)
    seq = pl.BlockSpec((1, 1, t, HG_DK), lambda bi, h: (bi, h, 0, 0))
    in_specs = [pl.BlockSpec((1, HG_DV), lambda bi, h: (0, 0))] + [seq] * 6
    args = [norm_g.reshape(1, HG_DV), q, kf, kb, gf, gb, v]
    st_spec = pl.BlockSpec((1, 2, 1, HG_DK, HG_DV), lambda bi, h: (bi, 0, h, 0, 0))
    if has_s0:
        in_specs.append(st_spec)
        args.append(s0)
    out_specs = [pl.BlockSpec((1, 1, t, HG_DV), lambda bi, h: (bi, h, 0, 0))]
    out_shape = [jax.ShapeDtypeStruct((b, nh, t, HG_DV), BF16)]
    if emit_state:
        out_specs.append(st_spec)
        out_shape.append(jax.ShapeDtypeStruct((b, 2, nh, HG_DK, HG_DV), F32))
    return pl.pallas_call(
        kern,
        grid=(b, nh),
        in_specs=in_specs,
        out_specs=out_specs,
        out_shape=out_shape,
        scratch_shapes=[pltpu.VMEM((nc, HG_DV, HG_DK), F32)] * 2 + [pltpu.VMEM((t, HG_DK), F32)] * 2,
        compiler_params=_cparams("parallel", "parallel"),
        name="gla",
    )(*args)


def _out_kernel(o_ref, g_ref, x_ref, w_ref, mod_ref, gpost_ref, y_ref):
    d = x_ref.shape[-1]
    o = jnp.concatenate([o_ref[0, h] for h in range(o_ref.shape[1])], axis=-1)
    y = _rms(_dot(o * g_ref[0], w_ref[...])) * gpost_ref[...]
    y_ref[0] = x_ref[0] + mod_ref[0, :, 2 * d:3 * d] * y


def _out_call(o, g, x, w, mod, gpost, tm):
    b, t, d = x.shape
    nh, dv = o.shape[1], o.shape[3]
    wd = nh * dv
    bm = mod.shape[0]
    mod_idx = (lambda bi, i: (bi, 0, 0)) if bm > 1 else (lambda bi, i: (0, 0, 0))
    return pl.pallas_call(
        _out_kernel,
        grid=(b, t // tm),
        in_specs=[
            pl.BlockSpec((1, nh, tm, dv), lambda bi, i: (bi, 0, i, 0)),
            pl.BlockSpec((1, tm, wd), lambda bi, i: (bi, i, 0)),
            pl.BlockSpec((1, tm, d), lambda bi, i: (bi, i, 0)),
            pl.BlockSpec((wd, d), lambda bi, i: (0, 0)),
            pl.BlockSpec((1, 1, mod.shape[2]), mod_idx),
            pl.BlockSpec((1, d), lambda bi, i: (0, 0)),
        ],
        out_specs=pl.BlockSpec((1, tm, d), lambda bi, i: (bi, i, 0)),
        out_shape=jax.ShapeDtypeStruct((b, t, d), F32),
        compiler_params=_cparams("parallel", "parallel"),
        name="out_proj",
    )(o, g, x, w, mod, gpost)


def _rope_tables(t):
    rows = t // GRID_W
    row = jnp.repeat(jnp.arange(rows), GRID_W)
    col = jnp.broadcast_to(jnp.arange(GRID_W), (rows, GRID_W)).reshape(-1)
    n_pair = DA_HEAD_DIM // 4
    inv = ROPE_BASE ** (-jnp.arange(n_pair, dtype=F32) / n_pair)
    ang = jnp.concatenate([row[:, None] * inv, col[:, None] * inv], axis=-1)
    cos = jnp.repeat(jnp.cos(ang), 2, axis=-1)
    sin = jnp.repeat(jnp.sin(ang), 2, axis=-1)
    odd = (jnp.arange(DA_HEAD_DIM) % 2) == 1
    sin_prev = jnp.where(odd, sin, 0.0)
    sin_next = jnp.where(odd, 0.0, -sin)
    tile = lambda a: jnp.tile(a, (1, LANES // DA_HEAD_DIM))
    return tile(cos), tile(sin_prev), tile(sin_next)


def kernel(x_prompt, x_sample, cache_k, cache_v, state_ret, state_hgrn, c, c_ctx,
           w_mod, b_mod, g_pre, g_post,
           da_w_in, da_w_out, da_lambda, da_subln,
           ret_w_in, ret_w_out, ret_decay,
           hg_w_in, hg_w_out, hg_lb, hg_norm):
    depth, d, _ = w_mod.shape
    db = x_sample.shape[0]
    ts = x_sample.shape[1]
    tp = x_prompt.shape[1]
    n_mix = 3
    kinds = [l % n_mix for l in range(depth)]
    slots = [kinds[:l].count(kinds[l]) for l in range(depth)]

    cvec = jnp.zeros((16, d), F32).at[:db].set(c).at[db].set(c_ctx)
    mods = _modulation_all(cvec, w_mod, b_mod)

    rope = _rope_tables(ts)
    tm = min(256, tp)
    tms = min(512, ts)
    tab_spec = pl.BlockSpec((tms, LANES), lambda bi, i: (i, 0))

    n_diff = kinds.count(0)
    past = cache_k.shape[2]
    cache_k = cache_k.reshape(db, n_diff, past, -1)
    cache_v = cache_v.reshape(db, n_diff, past, -1)
    xp, xs = x_prompt, x_sample
    new_k = new_v = None
    new_r, new_h = [], []
    for l in range(depth):
        kind, j = kinds[l], slots[l]
        mod_p = mods[l, db:db + 1].reshape(1, 1, 3 * d)
        mod_s = mods[l, :db].reshape(db, 1, 3 * d)
        gpre = g_pre[l].reshape(1, d)
        gpost = g_post[l].reshape(1, d)
        if kind == 0:
            w_in = da_w_in[j].astype(BF16)
            w_out = da_w_out[j].astype(BF16)
            heads = [("heads", LANES, BF16)] * 3 + [("plain", d, BF16)]
            carried = [] if new_k is None else [new_k, new_v]
            cache_outs = [("slot", d, F32, n_diff, j, new_k), ("slot", d, F32, n_diff, j, new_v)]
            qp, kp, vp, gp, new_k, new_v = _proj_call(
                functools.partial(_da_proj_kernel, rope=False, emit_cache=True,
                                  cache_slot=0 if carried else j, n_carried=len(carried)),
                xp, mod_p, gpre, w_in, [], [], heads + cache_outs, "da_proj_prompt", tm)
            qs, ks, vs, gs = _proj_call(
                functools.partial(_da_proj_kernel, rope=True, emit_cache=False),
                xs, mod_s, gpre, w_in, list(rope), [tab_spec] * 3, heads, "da_proj_sample", tms)
            op = _attention(qp, kp, vp, da_lambda[j], da_subln[j], l, tq=min(256, tp), hpu=N_HEADS)
            os_ = _attention(qs, ks, vs, da_lambda[j], da_subln[j], l, tq=min(512, ts), cache=(cache_k, cache_v, j))
        elif kind == 1:
            w_in = ret_w_in[j].astype(BF16)
            w_out = ret_w_out[j].astype(BF16)
            outs = [("heads", RET_DK, BF16)] * 2 + [("heads", RET_DV, BF16), ("plain", N_HEADS * RET_DV, BF16)]
            qp, kp, vp, gp = _proj_call(_ret_proj_kernel, xp, mod_p, gpre, w_in, [], [], outs,
                                        "ret_proj_prompt", tm)
            qs, ks, vs, gs = _proj_call(_ret_proj_kernel, xs, mod_s, gpre, w_in, [], [], outs,
                                        "ret_proj_sample", tms)
            dec = jnp.broadcast_to(ret_decay[j].T[:, :, None], (N_HEADS, 2, RET_DV))
            op, st = _retention(qp, kp, vp, dec, None, True)
            (os_,) = _retention(qs, ks, vs, dec, state_ret[:, j], False)
            new_r.append(st)
        else:
            w_in = hg_w_in[j].astype(BF16)
            w_out = hg_w_out[j].astype(BF16)
            lb = hg_lb.reshape(2 * depth, -1)
            lb_spec = pl.BlockSpec(lb.shape, lambda bi, i: (0, 0))
            outs = ([("heads", HG_DK, BF16)] * 3 + [("heads", HG_DK, F32)] * 2
                    + [("heads", HG_DV, BF16), ("plain", d, BF16)])
            kern = functools.partial(_hg_proj_kernel, layer_idx=l, depth=depth)
            pp = _proj_call(kern, xp, mod_p, gpre, w_in, [lb], [lb_spec], outs, "hg_proj_prompt", tm)
            ps = _proj_call(kern, xs, mod_s, gpre, w_in, [lb], [lb_spec], outs, "hg_proj_sample", tms)
            gp, gs = pp[6], ps[6]
            op, st = _gla(*pp[:6], hg_norm[j], None, True)
            (os_,) = _gla(*ps[:6], hg_norm[j], state_hgrn[:, j], False)
            new_h.append(st)
        xp = _out_call(op, gp, xp, w_out, mod_p, gpost, tm)
        xs = _out_call(os_, gs, xs, w_out, mod_s, gpost, tms)

    bp = x_prompt.shape[0]
    new_cache_k = new_k.reshape(bp, n_diff, tp, 2 * N_HEADS, DA_HEAD_DIM)
    new_cache_v = new_v.reshape(bp, n_diff, tp, N_HEADS, DA_V_DIM)
    stack = lambda parts: parts[0][:, None] if len(parts) == 1 else jnp.stack(parts, axis=1)
    new_state_ret = stack(new_r)
    new_state_hgrn = stack(new_h)
    return (xp, xs, new_cache_k, new_cache_v, new_state_ret, new_state_hgrn)
```

```python
import functools
import math

import jax
import jax.numpy as jnp
from jax import lax
from jax.experimental import pallas as pl
from jax.experimental.pallas import tpu as pltpu

F32 = jnp.float32
BF16 = jnp.bfloat16

EPS = 1e-6
LANES = 128
GRID_W = 64
ROPE_BASE = 10000.0
N_HEADS = 8
DA_HEAD_DIM = 64
DA_V_DIM = 128
RET_DK = 128
RET_DV = 256
HG_DK = 128
HG_DV = 128
RET_CHUNK = 256
GLA_CHUNK = 64
CHUNK_UNROLL = 16
VMEM_LIMIT = 56 * 1024 * 1024
Q_SCALE_LOG2 = DA_HEAD_DIM ** -0.5 * math.log2(math.e)


def _cparams(*sem):
    return pltpu.CompilerParams(dimension_semantics=sem, vmem_limit_bytes=VMEM_LIMIT)


def _dot(a, b):
    return jnp.dot(a, b, preferred_element_type=F32)


def _dot_nt(a, b):
    return lax.dot_general(a, b, (((1,), (1,)), ((), ())), preferred_element_type=F32)


def _dot_tn(a, b):
    return lax.dot_general(a, b, (((0,), (0,)), ((), ())), preferred_element_type=F32)


def _silu(x):
    return x * jax.nn.sigmoid(x)


def _rms(x):
    return x * lax.rsqrt(jnp.mean(x * x, axis=-1, keepdims=True) + EPS)


def _mod_kernel(c_ref, w_ref, b_ref, o_ref):
    s = _silu(c_ref[...]).astype(BF16)
    o_ref[0] = _dot(s, w_ref[0].astype(BF16)) + b_ref[0]


def _modulation_all(cvec, w_mod, b_mod):
    depth, d, n = w_mod.shape
    r = cvec.shape[0]
    tn = 1024
    return pl.pallas_call(
        _mod_kernel,
        grid=(depth, n // tn),
        in_specs=[
            pl.BlockSpec((r, d), lambda l, j: (0, 0)),
            pl.BlockSpec((1, d, tn), lambda l, j: (l, 0, j)),
            pl.BlockSpec((1, 1, tn), lambda l, j: (l, 0, j)),
        ],
        out_specs=pl.BlockSpec((1, r, tn), lambda l, j: (l, 0, j)),
        out_shape=jax.ShapeDtypeStruct((depth, r, n), F32),
        compiler_params=_cparams("parallel", "parallel"),
        name="modulation",
    )(cvec, w_mod, b_mod.reshape(depth, 1, n))


PROJ_NC = 512


def _norm_mod(x_ref, gpre_ref, mod_ref):
    d = x_ref.shape[-1]
    xn = _rms(x_ref[0]) * gpre_ref[...]
    shift = mod_ref[0, :, 0:d]
    scale = mod_ref[0, :, d:2 * d]
    return (xn * (1.0 + scale) + shift).astype(BF16)


def _store_heads(ref, acc, col0, dh, dtype):
    for j in range(acc.shape[1] // dh):
        ref[0, col0 // dh + j] = acc[:, j * dh:(j + 1) * dh].astype(dtype)


def _da_proj_kernel(*refs, rope, emit_cache, cache_slot=0, n_carried=0):
    x_ref, gpre_ref, mod_ref, w_ref = refs[:4]
    i = 4
    if rope:
        cos_ref, sinp_ref, sinn_ref = refs[i:i + 3]
        i += 3
    i += n_carried
    q_ref, k_ref, v_ref, g_ref = refs[i:i + 4]
    i += 4
    d = x_ref.shape[-1]
    if emit_cache:
        kc_ref, vc_ref = refs[i:i + 2]
        for ref in (kc_ref, vc_ref):
            for s in range(ref.shape[1]):
                if s != cache_slot:
                    ref[0, s] = jnp.zeros(ref.shape[2:], ref.dtype)
    h = _norm_mod(x_ref, gpre_ref, mod_ref)

    def rot(a):
        out = []
        for j in range(a.shape[1] // LANES):
            t = a[:, j * LANES:(j + 1) * LANES]
            out.append(t * cos_ref[...] + pltpu.roll(t, 1, 1) * sinp_ref[...]
                       + pltpu.roll(t, LANES - 1, 1) * sinn_ref[...])
        return jnp.concatenate(out, axis=1)

    for sec in range(4):
        for c0 in range(0, d, PROJ_NC):
            acc = _dot(h, w_ref[:, sec * d + c0: sec * d + c0 + PROJ_NC])
            if sec == 0:
                if rope:
                    acc = rot(acc)
                _store_heads(q_ref, acc * Q_SCALE_LOG2, c0, LANES, BF16)
            elif sec == 1:
                if emit_cache:
                    kc_ref[0, cache_slot, :, c0:c0 + PROJ_NC] = acc
                if rope:
                    acc = rot(acc)
                _store_heads(k_ref, acc, c0, LANES, BF16)
            elif sec == 2:
                if emit_cache:
                    vc_ref[0, cache_slot, :, c0:c0 + PROJ_NC] = acc
                _store_heads(v_ref, acc, c0, DA_V_DIM, BF16)
            else:
                _store_heads(g_ref, _silu(acc), c0, DA_V_DIM, BF16)


def _ret_proj_kernel(x_ref, gpre_ref, mod_ref, w_ref, q_ref, k_ref, v_ref, g_ref):
    d = x_ref.shape[-1]
    h = _norm_mod(x_ref, gpre_ref, mod_ref)
    qk = N_HEADS * RET_DK
    vw = N_HEADS * RET_DV
    for c0 in range(0, qk, PROJ_NC):
        _store_heads(q_ref, _dot(h, w_ref[:, c0:c0 + PROJ_NC]), c0, RET_DK, BF16)
    for c0 in range(0, qk, PROJ_NC):
        acc = _dot(h, w_ref[:, qk + c0: qk + c0 + PROJ_NC]) * (RET_DK ** -0.5)
        _store_heads(k_ref, acc, c0, RET_DK, BF16)
    for c0 in range(0, vw, PROJ_NC):
        _store_heads(v_ref, _dot(h, w_ref[:, 2 * qk + c0: 2 * qk + c0 + PROJ_NC]), c0, RET_DV, BF16)
    for c0 in range(0, vw, PROJ_NC):
        _store_heads(g_ref, _silu(_dot(h, w_ref[:, 2 * qk + vw + c0: 2 * qk + vw + c0 + PROJ_NC])), c0, RET_DV, BF16)


def _hg_proj_kernel(x_ref, gpre_ref, mod_ref, w_ref, lb_ref,
                    q_ref, kf_ref, kb_ref, gf_ref, gb_ref, v_ref, g_ref, *, layer_idx, depth):
    d = x_ref.shape[-1]
    h = _norm_mod(x_ref, gpre_ref, mod_ref)
    f_w = N_HEADS * HG_DK

    def lower_bound(direction):
        lg = lb_ref[direction * depth:(direction + 1) * depth, :]
        e = jnp.exp(lg - jnp.max(lg, axis=0, keepdims=True))
        sm = e / jnp.sum(e, axis=0, keepdims=True)
        cum = sm[0:1]
        for i in range(1, layer_idx + 1):
            cum = cum + sm[i:i + 1]
        return cum - sm[0:1]

    lbs = (lower_bound(0), lower_bound(1))
    for c0 in range(0, f_w, PROJ_NC):
        _store_heads(q_ref, _silu(_dot(h, w_ref[:, c0:c0 + PROJ_NC])), c0, HG_DK, BF16)
    for direction, (k_out, g_out) in enumerate(((kf_ref, gf_ref), (kb_ref, gb_ref))):
        base = (1 + direction) * f_w
        for c0 in range(0, f_w, PROJ_NC):
            acc = _dot(h, w_ref[:, base + c0: base + c0 + PROJ_NC])
            lbd = lbs[direction][:, c0:c0 + PROJ_NC]
            f = lbd + (1.0 - lbd) * jax.nn.sigmoid(acc)
            _store_heads(g_out, jnp.log(f), c0, HG_DK, F32)
            _store_heads(k_out, 1.0 - f, c0, HG_DK, BF16)
    for c0 in range(0, d, PROJ_NC):
        _store_heads(v_ref, _dot(h, w_ref[:, 3 * f_w + c0: 3 * f_w + c0 + PROJ_NC]), c0, HG_DV, BF16)
    for c0 in range(0, d, PROJ_NC):
        _store_heads(g_ref, _silu(_dot(h, w_ref[:, 3 * f_w + d + c0: 3 * f_w + d + c0 + PROJ_NC])), c0, HG_DV, BF16)


def _proj_call(kernel, x, mod, gpre, w, extra_in, extra_specs, outs, name, tm):
    b, t, d = x.shape
    n = w.shape[1]
    bm = mod.shape[0]
    mod_idx = (lambda bi, i: (bi, 0, 0)) if bm > 1 else (lambda bi, i: (0, 0, 0))
    in_specs = [
        pl.BlockSpec((1, tm, d), lambda bi, i: (bi, i, 0)),
        pl.BlockSpec((1, d), lambda bi, i: (0, 0)),
        pl.BlockSpec((1, 1, mod.shape[2]), mod_idx),
        pl.BlockSpec((d, n), lambda bi, i: (0, 0)),
    ] + extra_specs
    args = [x, gpre, mod, w, *extra_in]
    out_specs, out_shape, aliases = [], [], {}
    for out in outs:
        kind, width, dtype = out[:3]
        if kind == "heads":
            out_specs.append(pl.BlockSpec((1, N_HEADS, tm, width), lambda bi, i: (bi, 0, i, 0)))
            out_shape.append(jax.ShapeDtypeStruct((b, N_HEADS, t, width), dtype))
        elif kind == "plain":
            out_specs.append(pl.BlockSpec((1, tm, width), lambda bi, i: (bi, i, 0)))
            out_shape.append(jax.ShapeDtypeStruct((b, t, width), dtype))
        else:
            n_slots, slot, carried = out[3:]
            out_shape.append(jax.ShapeDtypeStruct((b, n_slots, t, width), dtype))
            if carried is None:
                out_specs.append(pl.BlockSpec((1, n_slots, tm, width), lambda bi, i: (bi, 0, i, 0)))
            else:
                out_specs.append(pl.BlockSpec((1, 1, tm, width), functools.partial(lambda bi, i, s: (bi, s, i, 0), s=slot)))
                aliases[len(args)] = len(out_shape) - 1
                in_specs.append(pl.BlockSpec(memory_space=pl.ANY))
                args.append(carried)
    return pl.pallas_call(
        kernel,
        grid=(b, t // tm),
        in_specs=in_specs,
        out_specs=out_specs,
        out_shape=out_shape,
        input_output_aliases=aliases,
        compiler_params=_cparams("parallel", "parallel"),
        name=name,
    )(*args)


def _attn_stages(lam_ref, subg_ref, q_ref, k_ref, v_ref, g_ref, ck_ref, cv_ref, o_ref, s_w, m_w, s_r, m_r,
                 *, tq, kc, nkc, nkc_own, hpu, lam_init):
    def chunk_of(own_ref, cache_ref, hh, c):
        if c < nkc_own:
            return own_ref[0, hh, c * kc:(c + 1) * kc, :]
        rows = slice((c - nkc_own) * kc, (c - nkc_own + 1) * kc)
        return cache_ref[0, 0, rows, hh * LANES:(hh + 1) * LANES].astype(BF16)

    lane = lax.broadcasted_iota(jnp.int32, (tq, LANES), 1)
    lf = lam_ref[...]
    lam = (jnp.exp(jnp.sum(lf[0:1] * lf[1:2], axis=-1, keepdims=True))
           - jnp.exp(jnp.sum(lf[2:3] * lf[3:4], axis=-1, keepdims=True)) + lam_init)
    for hh in range(hpu):
        q = q_ref[0, hh]
        zero = jnp.zeros_like(q)
        qs = jnp.concatenate([jnp.where(lane < DA_HEAD_DIM, q, zero),
                              jnp.where(lane >= DA_HEAD_DIM, q, zero)], axis=0)
        m = jnp.full((2 * tq, LANES), -jnp.inf, F32)
        for c in range(nkc):
            s = _dot_nt(qs, chunk_of(k_ref, ck_ref, hh, c))
            s_w[hh * nkc + c] = s
            for j in range(kc // LANES):
                m = jnp.maximum(m, s[:, j * LANES:(j + 1) * LANES])
        m_w[hh] = jnp.broadcast_to(jnp.max(m, axis=-1, keepdims=True), m.shape)

        m_row = m_r[hh]
        acc = jnp.zeros((2 * tq, 2 * DA_V_DIM), F32)
        for c in range(nkc):
            p = jnp.exp2(s_r[hh * nkc + c] - jnp.tile(m_row, (1, kc // LANES))).astype(BF16)
            v = chunk_of(v_ref, cv_ref, hh, c)
            acc = acc + _dot(p, jnp.concatenate([v, jnp.ones_like(v)], axis=1))
        on = acc[:, :DA_V_DIM] / acc[:, DA_V_DIM:]
        o = on[:tq] - lam * on[tq:]
        o = _rms(o) * subg_ref[...] * (1.0 - lam_init)
        o_ref[0, hh] = (o * g_ref[0, hh].astype(F32)).astype(BF16)


def _attn_kernel(*refs, has_cache, **static):
    if has_cache:
        lam_ref, subg_ref, q_ref, k_ref, v_ref, g_ref, ck_ref, cv_ref, o_ref, s0, s1, m0, m1 = refs
    else:
        lam_ref, subg_ref, q_ref, k_ref, v_ref, g_ref, o_ref, s0, s1, m0, m1 = refs
        ck_ref = cv_ref = None
    t = pl.program_id(0)

    @pl.when(t == 0)
    def _init():
        s1[...] = jnp.zeros(s1.shape, F32)
        m1[...] = jnp.zeros(m1.shape, F32)

    io = (lam_ref, subg_ref, q_ref, k_ref, v_ref, g_ref, ck_ref, cv_ref, o_ref)

    @pl.when(t % 2 == 0)
    def _even():
        _attn_stages(*io, s0, m0, s1, m1, **static)

    @pl.when(t % 2 == 1)
    def _odd():
        _attn_stages(*io, s1, m1, s0, m0, **static)


def _attention(q, k, v, g, lam_vec, sub_g, layer_idx, tq, hpu=1, cache=None):
    b, nh, t, _ = q.shape
    nh = nh // hpu
    past = 0 if cache is None else cache[0].shape[2]
    kc = next(w for w in (512, 256, 128) if t % w == 0 and past % w == 0)
    nkc_own = t // kc
    nkc = nkc_own + past // kc
    nq = t // tq
    n_units = b * nh * nq
    lam_init = 0.8 - 0.6 * math.exp(-0.3 * layer_idx)
    kern = functools.partial(_attn_kernel, has_cache=cache is not None, tq=tq, kc=kc, nkc=nkc,
                             nkc_own=nkc_own, hpu=hpu, lam_init=lam_init)

    def unit(u):
        return u // (nh * nq), (u // nq) % nh, u % nq

    def head_unit(step):
        return unit(jnp.minimum(step, n_units - 1))

    def tail_unit(step):
        return unit(jnp.maximum(step - 1, 0))

    def q_idx(step):
        bi, h, i = head_unit(step)
        return bi, h, i, 0

    def k_idx(step):
        bi, h, _ = head_unit(step)
        return bi, h, 0, 0

    def v_idx(step):
        bi, h, _ = tail_unit(step)
        return bi, h, 0, 0

    def o_idx(step):
        bi, h, i = tail_unit(step)
        return bi, h, i, 0

    in_specs = [
        pl.BlockSpec(lam_vec.shape, lambda step: (0, 0)),
        pl.BlockSpec((1, DA_V_DIM), lambda step: (0, 0)),
        pl.BlockSpec((1, hpu, tq, LANES), q_idx),
        pl.BlockSpec((1, hpu, t, LANES), k_idx),
        pl.BlockSpec((1, hpu, t, DA_V_DIM), v_idx),
        pl.BlockSpec((1, hpu, tq, DA_V_DIM), o_idx),
    ]
    args = [lam_vec, sub_g.reshape(1, DA_V_DIM), q, k, v, g]
    if cache is not None:
        cache_k, cache_v, slot = cache

        def ck_idx(step):
            bi, h, _ = head_unit(step)
            return bi, slot, 0, h

        def cv_idx(step):
            bi, h, _ = tail_unit(step)
            return bi, slot, 0, h

        in_specs += [pl.BlockSpec((1, 1, past, hpu * LANES), ck_idx),
                     pl.BlockSpec((1, 1, past, hpu * DA_V_DIM), cv_idx)]
        args += [cache_k, cache_v]
    return pl.pallas_call(
        kern,
        grid=(n_units + 1,),
        in_specs=in_specs,
        out_specs=pl.BlockSpec((1, hpu, tq, DA_V_DIM), o_idx),
        out_shape=jax.ShapeDtypeStruct(q.shape, BF16),
        scratch_shapes=[pltpu.VMEM((hpu * nkc, 2 * tq, kc), F32)] * 2 + [pltpu.VMEM((hpu, 2 * tq, LANES), F32)] * 2,
        compiler_params=_cparams("arbitrary"),
        name="diff_attention",
    )(*args)


def _ret_kernel(*refs, chunk, nc, hh, has_s0, emit_state):
    dec_ref, q_ref, k_ref, v_ref, g_ref = refs[:5]
    i = 5
    if has_s0:
        s0_ref = refs[i]
        i += 1
    o_ref = refs[i]
    i += 1
    if emit_state:
        st_ref = refs[i]
        i += 1
    sf_ref, sb_ref = refs[i:i + 2]
    c_ = chunk
    lg = jnp.log1p(-jnp.exp(dec_ref[hh]))
    lgf, lgb = lg[0:1, :RET_DK], lg[1:2, :RET_DK]
    row = lax.broadcasted_iota(jnp.int32, (c_, RET_DK), 0).astype(F32)
    k_dec_f = jnp.exp((c_ - 1.0 - row) * lgf)
    k_dec_b = jnp.exp(row * lgb)
    q_dec_f = jnp.exp((row + 1.0) * lgf)
    q_dec_b = jnp.exp((c_ - row) * lgb)
    chunk_dec_f = jnp.exp(c_ * lg[0:1])
    chunk_dec_b = jnp.exp(c_ * lg[1:2])
    diff = (lax.broadcasted_iota(jnp.int32, (c_, c_), 0) - lax.broadcasted_iota(jnp.int32, (c_, c_), 1)).astype(F32)
    dmat = (jnp.where(diff >= 0, jnp.exp(jnp.where(diff >= 0, diff, 0.0) * lg[0:1, :c_]), 0.0)
            + jnp.where(diff <= 0, jnp.exp(jnp.where(diff <= 0, -diff, 0.0) * lg[1:2, :c_]), 0.0))

    def rows(c):
        return pl.ds(pl.multiple_of(c * c_, c_), c_)

    def chunk_kv(c, carry):
        kc = k_ref[0, hh, rows(c), :].astype(F32)
        vc = v_ref[0, hh, rows(c), :]
        kd = jnp.concatenate([(kc * k_dec_f).astype(BF16), (kc * k_dec_b).astype(BF16)], axis=1)
        u = _dot_tn(kd, vc)
        sf_ref[hh, c] = u[:RET_DK]
        sb_ref[hh, c] = u[RET_DK:]
        return carry

    lax.fori_loop(0, nc, chunk_kv, 0, unroll=min(nc, CHUNK_UNROLL))

    if has_s0:
        s_f, s_b = s0_ref[0, 0, hh], s0_ref[0, 1, hh]
    else:
        s_f = s_b = jnp.zeros((RET_DK, RET_DV), F32)

    def scan(j, carry):
        s_f, s_b = carry
        cb = nc - 1 - j
        u_f, u_b = sf_ref[hh, j], sb_ref[hh, cb]
        sf_ref[hh, j] = s_f
        sb_ref[hh, cb] = s_b
        return chunk_dec_f * s_f + u_f, chunk_dec_b * s_b + u_b

    s_f, s_b = lax.fori_loop(0, nc, scan, (s_f, s_b), unroll=nc <= 4)
    if emit_state:
        st_ref[0, 0, hh] = s_f
        st_ref[0, 1, hh] = s_b

    def chunk_out(c, carry):
        qc = q_ref[0, hh, rows(c), :]
        kc = k_ref[0, hh, rows(c), :]
        vc = v_ref[0, hh, rows(c), :]
        sc = (_dot_nt(qc, kc) * dmat).astype(BF16)
        qf = qc.astype(F32)
        lhs = jnp.concatenate([sc, (qf * q_dec_f).astype(BF16), (qf * q_dec_b).astype(BF16)], axis=1)
        rhs = jnp.concatenate([vc, sf_ref[hh, c].astype(BF16), sb_ref[hh, c].astype(BF16)], axis=0)
        o = _dot(lhs, rhs)
        o_ref[0, hh, rows(c), :] = (_rms(o) * g_ref[0, hh, rows(c), :].astype(F32)).astype(BF16)
        return carry

    lax.fori_loop(0, nc, chunk_out, 0, unroll=min(nc, CHUNK_UNROLL))


def _per_head(kernel, *refs, hps, **static):
    for hh in range(hps):
        kernel(*refs, hh=hh, **static)


def _retention(q, k, v, g, dec, s0, emit_state, hps=1):
    b, nh, t, _ = q.shape
    c_ = RET_CHUNK
    nc = t // c_
    has_s0 = s0 is not None
    kern = functools.partial(_per_head, _ret_kernel, hps=hps, chunk=c_, nc=nc, has_s0=has_s0,
                             emit_state=emit_state)
    in_specs = [
        pl.BlockSpec((hps, 2, RET_DV), lambda bi, h: (h, 0, 0)),
        pl.BlockSpec((1, hps, t, RET_DK), lambda bi, h: (bi, h, 0, 0)),
        pl.BlockSpec((1, hps, t, RET_DK), lambda bi, h: (bi, h, 0, 0)),
        pl.BlockSpec((1, hps, t, RET_DV), lambda bi, h: (bi, h, 0, 0)),
        pl.BlockSpec((1, hps, t, RET_DV), lambda bi, h: (bi, h, 0, 0)),
    ]
    args = [dec, q, k, v, g]
    st_spec = pl.BlockSpec((1, 2, hps, RET_DK, RET_DV), lambda bi, h: (bi, 0, h, 0, 0))
    if has_s0:
        in_specs.append(st_spec)
        args.append(s0)
    out_specs = [pl.BlockSpec((1, hps, t, RET_DV), lambda bi, h: (bi, h, 0, 0))]
    out_shape = [jax.ShapeDtypeStruct((b, nh, t, RET_DV), BF16)]
    if emit_state:
        out_specs.append(st_spec)
        out_shape.append(jax.ShapeDtypeStruct((b, 2, nh, RET_DK, RET_DV), F32))
    return pl.pallas_call(
        kern,
        grid=(b, nh // hps),
        in_specs=in_specs,
        out_specs=out_specs,
        out_shape=out_shape,
        scratch_shapes=[pltpu.VMEM((hps, nc, RET_DK, RET_DV), F32)] * 2,
        compiler_params=_cparams("parallel", "parallel"),
        name="retention",
    )(*args)


def _gla_kernel(*refs, chunk, nc, hh, has_s0, emit_state):
    ng_ref, q_ref, kf_ref, kb_ref, gf_ref, gb_ref, v_ref, sg_ref = refs[:8]
    i = 8
    if has_s0:
        s0_ref = refs[i]
        i += 1
    o_ref = refs[i]
    i += 1
    if emit_state:
        st_ref = refs[i]
        i += 1
    sf_ref, sb_ref, bf_ref, bb_ref = refs[i:i + 4]
    c_ = chunk
    mid = c_ // 2
    r = lax.broadcasted_iota(jnp.int32, (c_, c_), 0)
    cc = lax.broadcasted_iota(jnp.int32, (c_, c_), 1)
    tri_f = (r >= cc).astype(BF16)
    tri_b = (r <= cc).astype(BF16)
    r2 = lax.broadcasted_iota(jnp.int32, (c_, 2 * c_), 0)
    c2 = lax.broadcasted_iota(jnp.int32, (c_, 2 * c_), 1)
    mask_f = (c2 < c_) & (r2 >= c2)
    mask_b = (c2 >= c_) & (r2 <= c2 - c_)

    def rows(c):
        return pl.ds(pl.multiple_of(c * c_, c_), c_)

    group = min(nc, 8)
    for g_ref, tri, b_ref in ((gf_ref, tri_f, bf_ref), (gb_ref, tri_b, bb_ref)):
        for c0 in range(0, nc, group):
            g = jnp.concatenate([g_ref[0, hh, (c0 + j) * c_:(c0 + j + 1) * c_, :] for j in range(group)], axis=1)
            hi = g.astype(BF16)
            lo = (g - hi.astype(F32)).astype(BF16)
            b = _dot(tri, hi) + _dot(tri, lo)
            for j in range(group):
                b_ref[hh, (c0 + j) * c_:(c0 + j + 1) * c_, :] = b[:, j * HG_DK:(j + 1) * HG_DK]

    def chunk_kv(c, carry):
        bf = bf_ref[hh, rows(c), :]
        bb = bb_ref[hh, rows(c), :]
        kdf = (kf_ref[0, hh, rows(c), :].astype(F32) * jnp.exp(bf[c_ - 1:c_] - bf)).astype(BF16)
        kdb = (kb_ref[0, hh, rows(c), :].astype(F32) * jnp.exp(bb[0:1] - bb)).astype(BF16)
        u = _dot_tn(v_ref[0, hh, rows(c), :], jnp.concatenate([kdf, kdb], axis=1))
        sf_ref[hh, c] = u[:, :HG_DK]
        sb_ref[hh, c] = u[:, HG_DK:]
        return carry

    lax.fori_loop(0, nc, chunk_kv, 0, unroll=min(nc, CHUNK_UNROLL))

    if has_s0:
        s_f, s_b = s0_ref[0, 0, hh].T, s0_ref[0, 1, hh].T
    else:
        s_f = s_b = jnp.zeros((HG_DV, HG_DK), F32)

    def scan(j, carry):
        s_f, s_b = carry
        cb = nc - 1 - j
        u_f, u_b = sf_ref[hh, j], sb_ref[hh, cb]
        sf_ref[hh, j] = s_f
        sb_ref[hh, cb] = s_b
        last = pl.multiple_of(j * c_, c_) + (c_ - 1)
        first = pl.multiple_of(cb * c_, c_)
        return (jnp.exp(bf_ref[hh, pl.ds(last, 1), :]) * s_f + u_f,
                jnp.exp(bb_ref[hh, pl.ds(first, 1), :]) * s_b + u_b)

    s_f, s_b = lax.fori_loop(0, nc, scan, (s_f, s_b), unroll=nc <= 4)
    if emit_state:
        st_ref[0, 0, hh] = s_f.T
        st_ref[0, 1, hh] = s_b.T

    def chunk_out(c, carry):
        q = q_ref[0, hh, rows(c), :].astype(F32)
        kf = kf_ref[0, hh, rows(c), :].astype(F32)
        kb = kb_ref[0, hh, rows(c), :].astype(F32)
        v = v_ref[0, hh, rows(c), :]
        bf = bf_ref[hh, rows(c), :]
        bb = bb_ref[hh, rows(c), :]
        ref_f = bf[mid:mid + 1]
        ref_b = bb[c_ - 1 - mid:c_ - mid]
        qq = jnp.concatenate([(q * jnp.exp(bf - ref_f)).astype(BF16), (q * jnp.exp(bb - ref_b)).astype(BF16)], axis=0)
        kk = jnp.concatenate([(kf * jnp.exp(ref_f - bf)).astype(BF16), (kb * jnp.exp(ref_b - bb)).astype(BF16)], axis=0)
        pr = _dot_nt(qq, kk)
        sc = jnp.where(mask_f, pr[:c_], 0.0) + jnp.where(mask_b, pr[c_:], 0.0)
        qcat = jnp.concatenate([(q * jnp.exp(bf)).astype(BF16), (q * jnp.exp(bb)).astype(BF16)], axis=1)
        scat = jnp.concatenate([sf_ref[hh, c], sb_ref[hh, c]], axis=1).astype(BF16)
        o = _dot(sc.astype(BF16), jnp.concatenate([v, v], axis=0)) + _dot_nt(qcat, scat)
        o_ref[0, hh, rows(c), :] = (_rms(o) * ng_ref[...] * sg_ref[0, hh, rows(c), :].astype(F32)).astype(BF16)
        return carry

    lax.fori_loop(0, nc, chunk_out, 0, unroll=min(nc, CHUNK_UNROLL))


def _gla(q, kf, kb, gf, gb, v, g, norm_g, s0, emit_state, hps=1):
    b, nh, t, _ = q.shape
    c_ = GLA_CHUNK
    nc = t // c_
    has_s0 = s0 is not None
    kern = functools.partial(_per_head, _gla_kernel, hps=hps, chunk=c_, nc=nc, has_s0=has_s0,
                             emit_state=emit_state)
    seq = pl.BlockSpec((1, hps, t, HG_DK), lambda bi, h: (bi, h, 0, 0))
    in_specs = [pl.BlockSpec((1, HG_DV), lambda bi, h: (0, 0))] + [seq] * 7
    args = [norm_g.reshape(1, HG_DV), q, kf, kb, gf, gb, v, g]
    st_spec = pl.BlockSpec((1, 2, hps, HG_DK, HG_DV), lambda bi, h: (bi, 0, h, 0, 0))
    if has_s0:
        in_specs.append(st_spec)
        args.append(s0)
    out_specs = [pl.BlockSpec((1, hps, t, HG_DV), lambda bi, h: (bi, h, 0, 0))]
    out_shape = [jax.ShapeDtypeStruct((b, nh, t, HG_DV), BF16)]
    if emit_state:
        out_specs.append(st_spec)
        out_shape.append(jax.ShapeDtypeStruct((b, 2, nh, HG_DK, HG_DV), F32))
    return pl.pallas_call(
        kern,
        grid=(b, nh // hps),
        in_specs=in_specs,
        out_specs=out_specs,
        out_shape=out_shape,
        scratch_shapes=[pltpu.VMEM((hps, nc, HG_DV, HG_DK), F32)] * 2 + [pltpu.VMEM((hps, t, HG_DK), F32)] * 2,
        compiler_params=_cparams("parallel", "parallel"),
        name="gla",
    )(*args)


def _out_kernel(o_ref, x_ref, w_ref, mod_ref, gpost_ref, y_ref):
    d = x_ref.shape[-1]
    o = jnp.concatenate([o_ref[0, h] for h in range(o_ref.shape[1])], axis=-1)
    y = _rms(_dot(o, w_ref[...])) * gpost_ref[...]
    y_ref[0] = x_ref[0] + mod_ref[0, :, 2 * d:3 * d] * y


def _out_call(o, x, w, mod, gpost, tm):
    b, t, d = x.shape
    nh, dv = o.shape[1], o.shape[3]
    wd = nh * dv
    bm = mod.shape[0]
    mod_idx = (lambda bi, i: (bi, 0, 0)) if bm > 1 else (lambda bi, i: (0, 0, 0))
    return pl.pallas_call(
        _out_kernel,
        grid=(b, t // tm),
        in_specs=[
            pl.BlockSpec((1, nh, tm, dv), lambda bi, i: (bi, 0, i, 0)),
            pl.BlockSpec((1, tm, d), lambda bi, i: (bi, i, 0)),
            pl.BlockSpec((wd, d), lambda bi, i: (0, 0)),
            pl.BlockSpec((1, 1, mod.shape[2]), mod_idx),
            pl.BlockSpec((1, d), lambda bi, i: (0, 0)),
        ],
        out_specs=pl.BlockSpec((1, tm, d), lambda bi, i: (bi, i, 0)),
        out_shape=jax.ShapeDtypeStruct((b, t, d), F32),
        compiler_params=_cparams("parallel", "parallel"),
        name="out_proj",
    )(o, x, w, mod, gpost)


def _rope_tables(t):
    rows = t // GRID_W
    row = jnp.repeat(jnp.arange(rows), GRID_W)
    col = jnp.broadcast_to(jnp.arange(GRID_W), (rows, GRID_W)).reshape(-1)
    n_pair = DA_HEAD_DIM // 4
    inv = ROPE_BASE ** (-jnp.arange(n_pair, dtype=F32) / n_pair)
    ang = jnp.concatenate([row[:, None] * inv, col[:, None] * inv], axis=-1)
    cos = jnp.repeat(jnp.cos(ang), 2, axis=-1)
    sin = jnp.repeat(jnp.sin(ang), 2, axis=-1)
    odd = (jnp.arange(DA_HEAD_DIM) % 2) == 1
    sin_prev = jnp.where(odd, sin, 0.0)
    sin_next = jnp.where(odd, 0.0, -sin)
    tile = lambda a: jnp.tile(a, (1, LANES // DA_HEAD_DIM))
    return tile(cos), tile(sin_prev), tile(sin_next)


def kernel(x_prompt, x_sample, cache_k, cache_v, state_ret, state_hgrn, c, c_ctx,
           w_mod, b_mod, g_pre, g_post,
           da_w_in, da_w_out, da_lambda, da_subln,
           ret_w_in, ret_w_out, ret_decay,
           hg_w_in, hg_w_out, hg_lb, hg_norm):
    depth, d, _ = w_mod.shape
    db = x_sample.shape[0]
    ts = x_sample.shape[1]
    tp = x_prompt.shape[1]
    n_mix = 3
    kinds = [l % n_mix for l in range(depth)]
    slots = [kinds[:l].count(kinds[l]) for l in range(depth)]

    cvec = jnp.zeros((16, d), F32).at[:db].set(c).at[db].set(c_ctx)
    mods = _modulation_all(cvec, w_mod, b_mod)

    rope = _rope_tables(ts)
    tm = min(256, tp)
    tms = min(512, ts)
    tab_spec = pl.BlockSpec((tms, LANES), lambda bi, i: (i, 0))

    n_diff = kinds.count(0)
    past = cache_k.shape[2]
    cache_k = cache_k.reshape(db, n_diff, past, -1)
    cache_v = cache_v.reshape(db, n_diff, past, -1)
    xp, xs = x_prompt, x_sample
    new_k = new_v = None
    new_r, new_h = [], []
    for l in range(depth):
        kind, j = kinds[l], slots[l]
        mod_p = mods[l, db:db + 1].reshape(1, 1, 3 * d)
        mod_s = mods[l, :db].reshape(db, 1, 3 * d)
        gpre = g_pre[l].reshape(1, d)
        gpost = g_post[l].reshape(1, d)
        if kind == 0:
            w_in = da_w_in[j].astype(BF16)
            w_out = da_w_out[j].astype(BF16)
            heads = [("heads", LANES, BF16)] * 4
            carried = [] if new_k is None else [new_k, new_v]
            cache_outs = [("slot", d, F32, n_diff, j, new_k), ("slot", d, F32, n_diff, j, new_v)]
            qp, kp, vp, gp, new_k, new_v = _proj_call(
                functools.partial(_da_proj_kernel, rope=False, emit_cache=True,
                                  cache_slot=0 if carried else j, n_carried=len(carried)),
                xp, mod_p, gpre, w_in, [], [], heads + cache_outs, "da_proj_prompt", tm)
            qs, ks, vs, gs = _proj_call(
                functools.partial(_da_proj_kernel, rope=True, emit_cache=False),
                xs, mod_s, gpre, w_in, list(rope), [tab_spec] * 3, heads, "da_proj_sample", tms)
            op = _attention(qp, kp, vp, gp, da_lambda[j], da_subln[j], l, tq=min(256, tp), hpu=N_HEADS)
            os_ = _attention(qs, ks, vs, gs, da_lambda[j], da_subln[j], l, tq=min(512, ts), cache=(cache_k, cache_v, j))
        elif kind == 1:
            w_in = ret_w_in[j].astype(BF16)
            w_out = ret_w_out[j].astype(BF16)
            outs = [("heads", RET_DK, BF16)] * 2 + [("heads", RET_DV, BF16)] * 2
            qp, kp, vp, gp = _proj_call(_ret_proj_kernel, xp, mod_p, gpre, w_in, [], [], outs,
                                        "ret_proj_prompt", tm)
            qs, ks, vs, gs = _proj_call(_ret_proj_kernel, xs, mod_s, gpre, w_in, [], [], outs,
                                        "ret_proj_sample", tms)
            dec = jnp.broadcast_to(ret_decay[j].T[:, :, None], (N_HEADS, 2, RET_DV))
            op, st = _retention(qp, kp, vp, gp, dec, None, True, hps=N_HEADS)
            (os_,) = _retention(qs, ks, vs, gs, dec, state_ret[:, j], False)
            new_r.append(st)
        else:
            w_in = hg_w_in[j].astype(BF16)
            w_out = hg_w_out[j].astype(BF16)
            lb = hg_lb.reshape(2 * depth, -1)
            lb_spec = pl.BlockSpec(lb.shape, lambda bi, i: (0, 0))
            outs = ([("heads", HG_DK, BF16)] * 3 + [("heads", HG_DK, F32)] * 2
                    + [("heads", HG_DV, BF16)] * 2)
            kern = functools.partial(_hg_proj_kernel, layer_idx=l, depth=depth)
            pp = _proj_call(kern, xp, mod_p, gpre, w_in, [lb], [lb_spec], outs, "hg_proj_prompt", tm)
            ps = _proj_call(kern, xs, mod_s, gpre, w_in, [lb], [lb_spec], outs, "hg_proj_sample", tms)
            op, st = _gla(*pp, hg_norm[j], None, True, hps=N_HEADS)
            (os_,) = _gla(*ps, hg_norm[j], state_hgrn[:, j], False)
            new_h.append(st)
        xp = _out_call(op, xp, w_out, mod_p, gpost, tm)
        xs = _out_call(os_, xs, w_out, mod_s, gpost, tms)

    bp = x_prompt.shape[0]
    new_cache_k = new_k.reshape(bp, n_diff, tp, 2 * N_HEADS, DA_HEAD_DIM)
    new_cache_v = new_v.reshape(bp, n_diff, tp, N_HEADS, DA_V_DIM)
    stack = lambda parts: parts[0][:, None] if len(parts) == 1 else jnp.stack(parts, axis=1)
    new_state_ret = stack(new_r)
    new_state_hgrn = stack(new_h)
    return (xp, xs, new_cache_k, new_cache_v, new_state_ret, new_state_hgrn)
```

```python
import functools
import math

import jax
import jax.numpy as jnp
from jax import lax
from jax.experimental import pallas as pl
from jax.experimental.pallas import tpu as pltpu

F32 = jnp.float32
BF16 = jnp.bfloat16

EPS = 1e-6
LANES = 128
GRID_W = 64
ROPE_BASE = 10000.0
N_HEADS = 8
DA_HEAD_DIM = 64
DA_V_DIM = 128
RET_DK = 128
RET_DV = 256
HG_DK = 128
HG_DV = 128
RET_CHUNK = 256
GLA_CHUNK = 64
CHUNK_UNROLL = 16
VMEM_LIMIT = 56 * 1024 * 1024
Q_SCALE_LOG2 = DA_HEAD_DIM ** -0.5 * math.log2(math.e)


def _cparams(*sem):
    return pltpu.CompilerParams(dimension_semantics=sem, vmem_limit_bytes=VMEM_LIMIT)


def _dot(a, b):
    return jnp.dot(a, b, preferred_element_type=F32)


def _dot_nt(a, b):
    return lax.dot_general(a, b, (((1,), (1,)), ((), ())), preferred_element_type=F32)


def _dot_tn(a, b):
    return lax.dot_general(a, b, (((0,), (0,)), ((), ())), preferred_element_type=F32)


def _silu(x):
    return x * jax.nn.sigmoid(x)


def _rms(x):
    return x * lax.rsqrt(jnp.mean(x * x, axis=-1, keepdims=True) + EPS)


def _mod_kernel(c_ref, w_ref, b_ref, o_ref):
    s = _silu(c_ref[...]).astype(BF16)
    o_ref[0] = _dot(s, w_ref[0].astype(BF16)) + b_ref[0]


def _modulation_all(cvec, w_mod, b_mod):
    depth, d, n = w_mod.shape
    r = cvec.shape[0]
    tn = 1024
    return pl.pallas_call(
        _mod_kernel,
        grid=(depth, n // tn),
        in_specs=[
            pl.BlockSpec((r, d), lambda l, j: (0, 0)),
            pl.BlockSpec((1, d, tn), lambda l, j: (l, 0, j)),
            pl.BlockSpec((1, 1, tn), lambda l, j: (l, 0, j)),
        ],
        out_specs=pl.BlockSpec((1, r, tn), lambda l, j: (l, 0, j)),
        out_shape=jax.ShapeDtypeStruct((depth, r, n), F32),
        compiler_params=_cparams("parallel", "parallel"),
        name="modulation",
    )(cvec, w_mod, b_mod.reshape(depth, 1, n))


PROJ_NC = 512


def _norm_mod(x_ref, gpre_ref, mod_ref):
    d = x_ref.shape[-1]
    xn = _rms(x_ref[0]) * gpre_ref[...]
    shift = mod_ref[0, :, 0:d]
    scale = mod_ref[0, :, d:2 * d]
    return (xn * (1.0 + scale) + shift).astype(BF16)


def _store_heads(ref, acc, col0, dh, dtype):
    for j in range(acc.shape[1] // dh):
        ref[0, col0 // dh + j] = acc[:, j * dh:(j + 1) * dh].astype(dtype)


def _da_proj_kernel(*refs, rope, emit_cache, cache_slot=0, n_carried=0):
    x_ref, gpre_ref, mod_ref, w_ref = refs[:4]
    i = 4
    if rope:
        cos_ref, sinp_ref, sinn_ref = refs[i:i + 3]
        i += 3
    i += n_carried
    q_ref, k_ref, v_ref, g_ref = refs[i:i + 4]
    i += 4
    d = x_ref.shape[-1]
    if emit_cache:
        kc_ref, vc_ref = refs[i:i + 2]
        for ref in (kc_ref, vc_ref):
            for s in range(ref.shape[1]):
                if s != cache_slot:
                    ref[0, s] = jnp.zeros(ref.shape[2:], ref.dtype)
    h = _norm_mod(x_ref, gpre_ref, mod_ref)

    def rot(a):
        out = []
        for j in range(a.shape[1] // LANES):
            t = a[:, j * LANES:(j + 1) * LANES]
            out.append(t * cos_ref[...] + pltpu.roll(t, 1, 1) * sinp_ref[...]
                       + pltpu.roll(t, LANES - 1, 1) * sinn_ref[...])
        return jnp.concatenate(out, axis=1)

    for sec in range(4):
        for c0 in range(0, d, PROJ_NC):
            acc = _dot(h, w_ref[:, sec * d + c0: sec * d + c0 + PROJ_NC])
            if sec == 0:
                if rope:
                    acc = rot(acc)
                _store_heads(q_ref, acc * Q_SCALE_LOG2, c0, LANES, BF16)
            elif sec == 1:
                if emit_cache:
                    kc_ref[0, cache_slot, :, c0:c0 + PROJ_NC] = acc
                if rope:
                    acc = rot(acc)
                _store_heads(k_ref, acc, c0, LANES, BF16)
            elif sec == 2:
                if emit_cache:
                    vc_ref[0, cache_slot, :, c0:c0 + PROJ_NC] = acc
                _store_heads(v_ref, acc, c0, DA_V_DIM, BF16)
            else:
                g_ref[0, :, c0:c0 + PROJ_NC] = _silu(acc).astype(BF16)


def _ret_proj_kernel(x_ref, gpre_ref, mod_ref, w_ref, q_ref, k_ref, v_ref, g_ref):
    d = x_ref.shape[-1]
    h = _norm_mod(x_ref, gpre_ref, mod_ref)
    qk = N_HEADS * RET_DK
    vw = N_HEADS * RET_DV
    for c0 in range(0, qk, PROJ_NC):
        _store_heads(q_ref, _dot(h, w_ref[:, c0:c0 + PROJ_NC]), c0, RET_DK, BF16)
    for c0 in range(0, qk, PROJ_NC):
        acc = _dot(h, w_ref[:, qk + c0: qk + c0 + PROJ_NC]) * (RET_DK ** -0.5)
        _store_heads(k_ref, acc, c0, RET_DK, BF16)
    for c0 in range(0, vw, PROJ_NC):
        _store_heads(v_ref, _dot(h, w_ref[:, 2 * qk + c0: 2 * qk + c0 + PROJ_NC]), c0, RET_DV, BF16)
    for c0 in range(0, vw, PROJ_NC):
        g_ref[0, :, c0:c0 + PROJ_NC] = _silu(_dot(
            h, w_ref[:, 2 * qk + vw + c0: 2 * qk + vw + c0 + PROJ_NC])).astype(BF16)


def _hg_proj_kernel(x_ref, gpre_ref, mod_ref, w_ref, lb_ref,
                    q_ref, kf_ref, kb_ref, gf_ref, gb_ref, v_ref, g_ref, *, layer_idx, depth):
    d = x_ref.shape[-1]
    h = _norm_mod(x_ref, gpre_ref, mod_ref)
    f_w = N_HEADS * HG_DK

    def lower_bound(direction):
        lg = lb_ref[direction * depth:(direction + 1) * depth, :]
        e = jnp.exp(lg - jnp.max(lg, axis=0, keepdims=True))
        sm = e / jnp.sum(e, axis=0, keepdims=True)
        cum = sm[0:1]
        for i in range(1, layer_idx + 1):
            cum = cum + sm[i:i + 1]
        return cum - sm[0:1]

    lbs = (lower_bound(0), lower_bound(1))
    for c0 in range(0, f_w, PROJ_NC):
        _store_heads(q_ref, _silu(_dot(h, w_ref[:, c0:c0 + PROJ_NC])), c0, HG_DK, BF16)
    for direction, (k_out, g_out) in enumerate(((kf_ref, gf_ref), (kb_ref, gb_ref))):
        base = (1 + direction) * f_w
        for c0 in range(0, f_w, PROJ_NC):
            acc = _dot(h, w_ref[:, base + c0: base + c0 + PROJ_NC])
            lbd = lbs[direction][:, c0:c0 + PROJ_NC]
            f = lbd + (1.0 - lbd) * jax.nn.sigmoid(acc)
            _store_heads(g_out, jnp.log(f), c0, HG_DK, F32)
            _store_heads(k_out, 1.0 - f, c0, HG_DK, BF16)
    for c0 in range(0, d, PROJ_NC):
        _store_heads(v_ref, _dot(h, w_ref[:, 3 * f_w + c0: 3 * f_w + c0 + PROJ_NC]), c0, HG_DV, BF16)
    for c0 in range(0, d, PROJ_NC):
        g_ref[0, :, c0:c0 + PROJ_NC] = _silu(_dot(
            h, w_ref[:, 3 * f_w + d + c0: 3 * f_w + d + c0 + PROJ_NC])).astype(BF16)


def _proj_call(kernel, x, mod, gpre, w, extra_in, extra_specs, outs, name, tm):
    b, t, d = x.shape
    n = w.shape[1]
    bm = mod.shape[0]
    mod_idx = (lambda bi, i: (bi, 0, 0)) if bm > 1 else (lambda bi, i: (0, 0, 0))
    in_specs = [
        pl.BlockSpec((1, tm, d), lambda bi, i: (bi, i, 0)),
        pl.BlockSpec((1, d), lambda bi, i: (0, 0)),
        pl.BlockSpec((1, 1, mod.shape[2]), mod_idx),
        pl.BlockSpec((d, n), lambda bi, i: (0, 0)),
    ] + extra_specs
    args = [x, gpre, mod, w, *extra_in]
    out_specs, out_shape, aliases = [], [], {}
    for out in outs:
        kind, width, dtype = out[:3]
        if kind == "heads":
            out_specs.append(pl.BlockSpec((1, N_HEADS, tm, width), lambda bi, i: (bi, 0, i, 0)))
            out_shape.append(jax.ShapeDtypeStruct((b, N_HEADS, t, width), dtype))
        elif kind == "plain":
            out_specs.append(pl.BlockSpec((1, tm, width), lambda bi, i: (bi, i, 0)))
            out_shape.append(jax.ShapeDtypeStruct((b, t, width), dtype))
        else:
            n_slots, slot, carried = out[3:]
            out_shape.append(jax.ShapeDtypeStruct((b, n_slots, t, width), dtype))
            if carried is None:
                out_specs.append(pl.BlockSpec((1, n_slots, tm, width), lambda bi, i: (bi, 0, i, 0)))
            else:
                out_specs.append(pl.BlockSpec((1, 1, tm, width), functools.partial(lambda bi, i, s: (bi, s, i, 0), s=slot)))
                aliases[len(args)] = len(out_shape) - 1
                in_specs.append(pl.BlockSpec(memory_space=pl.ANY))
                args.append(carried)
    return pl.pallas_call(
        kernel,
        grid=(b, t // tm),
        in_specs=in_specs,
        out_specs=out_specs,
        out_shape=out_shape,
        input_output_aliases=aliases,
        compiler_params=_cparams("parallel", "parallel"),
        name=name,
    )(*args)


def _attn_stages(lam_ref, subg_ref, q_ref, k_ref, v_ref, ck_ref, cv_ref, o_ref, s_w, m_w, s_r, m_r,
                 *, tq, kc, nkc, nkc_own, hpu, lam_init):
    def chunk_of(own_ref, cache_ref, hh, c):
        if c < nkc_own:
            return own_ref[0, hh, c * kc:(c + 1) * kc, :]
        rows = slice((c - nkc_own) * kc, (c - nkc_own + 1) * kc)
        return cache_ref[0, 0, rows, hh * LANES:(hh + 1) * LANES].astype(BF16)

    lane = lax.broadcasted_iota(jnp.int32, (tq, LANES), 1)
    lf = lam_ref[...]
    lam = (jnp.exp(jnp.sum(lf[0:1] * lf[1:2], axis=-1, keepdims=True))
           - jnp.exp(jnp.sum(lf[2:3] * lf[3:4], axis=-1, keepdims=True)) + lam_init)
    for hh in range(hpu):
        q = q_ref[0, hh]
        zero = jnp.zeros_like(q)
        qs = jnp.concatenate([jnp.where(lane < DA_HEAD_DIM, q, zero),
                              jnp.where(lane >= DA_HEAD_DIM, q, zero)], axis=0)
        m = jnp.full((2 * tq, LANES), -jnp.inf, F32)
        for c in range(nkc):
            s = _dot_nt(qs, chunk_of(k_ref, ck_ref, hh, c))
            s_w[hh * nkc + c] = s
            for j in range(kc // LANES):
                m = jnp.maximum(m, s[:, j * LANES:(j + 1) * LANES])
        m_w[hh] = jnp.broadcast_to(jnp.max(m, axis=-1, keepdims=True), m.shape)

        m_row = m_r[hh]
        acc = jnp.zeros((2 * tq, 2 * DA_V_DIM), F32)
        for c in range(nkc):
            p = jnp.exp2(s_r[hh * nkc + c] - jnp.tile(m_row, (1, kc // LANES))).astype(BF16)
            v = chunk_of(v_ref, cv_ref, hh, c)
            acc = acc + _dot(p, jnp.concatenate([v, jnp.ones_like(v)], axis=1))
        on = acc[:, :DA_V_DIM] / acc[:, DA_V_DIM:]
        o = on[:tq] - lam * on[tq:]
        o = _rms(o) * subg_ref[...] * (1.0 - lam_init)
        o_ref[0, hh] = o.astype(BF16)


def _attn_kernel(*refs, has_cache, **static):
    if has_cache:
        lam_ref, subg_ref, q_ref, k_ref, v_ref, ck_ref, cv_ref, o_ref, s0, s1, m0, m1 = refs
    else:
        lam_ref, subg_ref, q_ref, k_ref, v_ref, o_ref, s0, s1, m0, m1 = refs
        ck_ref = cv_ref = None
    t = pl.program_id(0)

    @pl.when(t == 0)
    def _init():
        s1[...] = jnp.zeros(s1.shape, F32)
        m1[...] = jnp.zeros(m1.shape, F32)

    io = (lam_ref, subg_ref, q_ref, k_ref, v_ref, ck_ref, cv_ref, o_ref)

    @pl.when(t % 2 == 0)
    def _even():
        _attn_stages(*io, s0, m0, s1, m1, **static)

    @pl.when(t % 2 == 1)
    def _odd():
        _attn_stages(*io, s1, m1, s0, m0, **static)


def _attention(q, k, v, lam_vec, sub_g, layer_idx, tq, hpu=1, cache=None):
    b, nh, t, _ = q.shape
    nh = nh // hpu
    past = 0 if cache is None else cache[0].shape[2]
    kc = next(w for w in (512, 256, 128) if t % w == 0 and past % w == 0)
    nkc_own = t // kc
    nkc = nkc_own + past // kc
    nq = t // tq
    n_units = b * nh * nq
    lam_init = 0.8 - 0.6 * math.exp(-0.3 * layer_idx)
    kern = functools.partial(_attn_kernel, has_cache=cache is not None, tq=tq, kc=kc, nkc=nkc,
                             nkc_own=nkc_own, hpu=hpu, lam_init=lam_init)

    def unit(u):
        return u // (nh * nq), (u // nq) % nh, u % nq

    def head_unit(step):
        return unit(jnp.minimum(step, n_units - 1))

    def tail_unit(step):
        return unit(jnp.maximum(step - 1, 0))

    def q_idx(step):
        bi, h, i = head_unit(step)
        return bi, h, i, 0

    def k_idx(step):
        bi, h, _ = head_unit(step)
        return bi, h, 0, 0

    def v_idx(step):
        bi, h, _ = tail_unit(step)
        return bi, h, 0, 0

    def o_idx(step):
        bi, h, i = tail_unit(step)
        return bi, h, i, 0

    in_specs = [
        pl.BlockSpec(lam_vec.shape, lambda step: (0, 0)),
        pl.BlockSpec((1, DA_V_DIM), lambda step: (0, 0)),
        pl.BlockSpec((1, hpu, tq, LANES), q_idx),
        pl.BlockSpec((1, hpu, t, LANES), k_idx),
        pl.BlockSpec((1, hpu, t, DA_V_DIM), v_idx),
    ]
    args = [lam_vec, sub_g.reshape(1, DA_V_DIM), q, k, v]
    if cache is not None:
        cache_k, cache_v, slot = cache

        def ck_idx(step):
            bi, h, _ = head_unit(step)
            return bi, slot, 0, h

        def cv_idx(step):
            bi, h, _ = tail_unit(step)
            return bi, slot, 0, h

        in_specs += [pl.BlockSpec((1, 1, past, hpu * LANES), ck_idx),
                     pl.BlockSpec((1, 1, past, hpu * DA_V_DIM), cv_idx)]
        args += [cache_k, cache_v]
    return pl.pallas_call(
        kern,
        grid=(n_units + 1,),
        in_specs=in_specs,
        out_specs=pl.BlockSpec((1, hpu, tq, DA_V_DIM), o_idx),
        out_shape=jax.ShapeDtypeStruct(q.shape, BF16),
        scratch_shapes=[pltpu.VMEM((hpu * nkc, 2 * tq, kc), F32)] * 2 + [pltpu.VMEM((hpu, 2 * tq, LANES), F32)] * 2,
        compiler_params=_cparams("arbitrary"),
        name="diff_attention",
    )(*args)


def _ret_kernel(*refs, chunk, nc, hh, has_s0, emit_state):
    dec_ref, q_ref, k_ref, v_ref = refs[:4]
    i = 4
    if has_s0:
        s0_ref = refs[i]
        i += 1
    o_ref = refs[i]
    i += 1
    if emit_state:
        st_ref = refs[i]
        i += 1
    sf_ref, sb_ref = refs[i:i + 2]
    c_ = chunk
    lg = jnp.log1p(-jnp.exp(dec_ref[hh]))
    lgf, lgb = lg[0:1, :RET_DK], lg[1:2, :RET_DK]
    row = lax.broadcasted_iota(jnp.int32, (c_, RET_DK), 0).astype(F32)
    k_dec_f = jnp.exp((c_ - 1.0 - row) * lgf)
    k_dec_b = jnp.exp(row * lgb)
    q_dec_f = jnp.exp((row + 1.0) * lgf)
    q_dec_b = jnp.exp((c_ - row) * lgb)
    chunk_dec_f = jnp.exp(c_ * lg[0:1])
    chunk_dec_b = jnp.exp(c_ * lg[1:2])
    diff = (lax.broadcasted_iota(jnp.int32, (c_, c_), 0) - lax.broadcasted_iota(jnp.int32, (c_, c_), 1)).astype(F32)
    dmat = (jnp.where(diff >= 0, jnp.exp(jnp.where(diff >= 0, diff, 0.0) * lg[0:1, :c_]), 0.0)
            + jnp.where(diff <= 0, jnp.exp(jnp.where(diff <= 0, -diff, 0.0) * lg[1:2, :c_]), 0.0))

    def rows(c):
        return pl.ds(pl.multiple_of(c * c_, c_), c_)

    def chunk_kv(c, carry):
        kc = k_ref[0, hh, rows(c), :].astype(F32)
        vc = v_ref[0, hh, rows(c), :]
        kd = jnp.concatenate([(kc * k_dec_f).astype(BF16), (kc * k_dec_b).astype(BF16)], axis=1)
        u = _dot_tn(kd, vc)
        sf_ref[hh, c] = u[:RET_DK]
        sb_ref[hh, c] = u[RET_DK:]
        return carry

    lax.fori_loop(0, nc, chunk_kv, 0, unroll=min(nc, CHUNK_UNROLL))

    if has_s0:
        s_f, s_b = s0_ref[0, 0, hh], s0_ref[0, 1, hh]
    else:
        s_f = s_b = jnp.zeros((RET_DK, RET_DV), F32)

    def scan(j, carry):
        s_f, s_b = carry
        cb = nc - 1 - j
        u_f, u_b = sf_ref[hh, j], sb_ref[hh, cb]
        sf_ref[hh, j] = s_f
        sb_ref[hh, cb] = s_b
        return chunk_dec_f * s_f + u_f, chunk_dec_b * s_b + u_b

    s_f, s_b = lax.fori_loop(0, nc, scan, (s_f, s_b), unroll=nc <= 4)
    if emit_state:
        st_ref[0, 0, hh] = s_f
        st_ref[0, 1, hh] = s_b

    def chunk_out(c, carry):
        qc = q_ref[0, hh, rows(c), :]
        kc = k_ref[0, hh, rows(c), :]
        vc = v_ref[0, hh, rows(c), :]
        sc = (_dot_nt(qc, kc) * dmat).astype(BF16)
        qf = qc.astype(F32)
        lhs = jnp.concatenate([sc, (qf * q_dec_f).astype(BF16), (qf * q_dec_b).astype(BF16)], axis=1)
        rhs = jnp.concatenate([vc, sf_ref[hh, c].astype(BF16), sb_ref[hh, c].astype(BF16)], axis=0)
        o = _dot(lhs, rhs)
        o_ref[0, hh, rows(c), :] = _rms(o).astype(BF16)
        return carry

    lax.fori_loop(0, nc, chunk_out, 0, unroll=min(nc, CHUNK_UNROLL))


def _per_head(kernel, *refs, hps, **static):
    for hh in range(hps):
        kernel(*refs, hh=hh, **static)


def _retention(q, k, v, dec, s0, emit_state, hps=1):
    b, nh, t, _ = q.shape
    c_ = RET_CHUNK
    nc = t // c_
    has_s0 = s0 is not None
    kern = functools.partial(_per_head, _ret_kernel, hps=hps, chunk=c_, nc=nc, has_s0=has_s0,
                             emit_state=emit_state)
    in_specs = [
        pl.BlockSpec((hps, 2, RET_DV), lambda bi, h: (h, 0, 0)),
        pl.BlockSpec((1, hps, t, RET_DK), lambda bi, h: (bi, h, 0, 0)),
        pl.BlockSpec((1, hps, t, RET_DK), lambda bi, h: (bi, h, 0, 0)),
        pl.BlockSpec((1, hps, t, RET_DV), lambda bi, h: (bi, h, 0, 0)),
    ]
    args = [dec, q, k, v]
    st_spec = pl.BlockSpec((1, 2, hps, RET_DK, RET_DV), lambda bi, h: (bi, 0, h, 0, 0))
    if has_s0:
        in_specs.append(st_spec)
        args.append(s0)
    out_specs = [pl.BlockSpec((1, hps, t, RET_DV), lambda bi, h: (bi, h, 0, 0))]
    out_shape = [jax.ShapeDtypeStruct((b, nh, t, RET_DV), BF16)]
    if emit_state:
        out_specs.append(st_spec)
        out_shape.append(jax.ShapeDtypeStruct((b, 2, nh, RET_DK, RET_DV), F32))
    return pl.pallas_call(
        kern,
        grid=(b, nh // hps),
        in_specs=in_specs,
        out_specs=out_specs,
        out_shape=out_shape,
        scratch_shapes=[pltpu.VMEM((hps, nc, RET_DK, RET_DV), F32)] * 2,
        compiler_params=_cparams("parallel", "parallel"),
        name="retention",
    )(*args)


def _gla_kernel(*refs, chunk, nc, hh, has_s0, emit_state):
    ng_ref, q_ref, kf_ref, kb_ref, gf_ref, gb_ref, v_ref = refs[:7]
    i = 7
    if has_s0:
        s0_ref = refs[i]
        i += 1
    o_ref = refs[i]
    i += 1
    if emit_state:
        st_ref = refs[i]
        i += 1
    sf_ref, sb_ref, bf_ref, bb_ref = refs[i:i + 4]
    c_ = chunk
    mid = c_ // 2
    r = lax.broadcasted_iota(jnp.int32, (c_, c_), 0)
    cc = lax.broadcasted_iota(jnp.int32, (c_, c_), 1)
    tri_f = (r >= cc).astype(BF16)
    tri_b = (r <= cc).astype(BF16)
    r2 = lax.broadcasted_iota(jnp.int32, (c_, 2 * c_), 0)
    c2 = lax.broadcasted_iota(jnp.int32, (c_, 2 * c_), 1)
    mask_f = (c2 < c_) & (r2 >= c2)
    mask_b = (c2 >= c_) & (r2 <= c2 - c_)

    def rows(c):
        return pl.ds(pl.multiple_of(c * c_, c_), c_)

    group = min(nc, 8)
    for g_ref, tri, b_ref in ((gf_ref, tri_f, bf_ref), (gb_ref, tri_b, bb_ref)):
        for c0 in range(0, nc, group):
            g = jnp.concatenate([g_ref[0, hh, (c0 + j) * c_:(c0 + j + 1) * c_, :] for j in range(group)], axis=1)
            hi = g.astype(BF16)
            lo = (g - hi.astype(F32)).astype(BF16)
            b = _dot(tri, hi) + _dot(tri, lo)
            for j in range(group):
                b_ref[hh, (c0 + j) * c_:(c0 + j + 1) * c_, :] = b[:, j * HG_DK:(j + 1) * HG_DK]

    def chunk_kv(c, carry):
        bf = bf_ref[hh, rows(c), :]
        bb = bb_ref[hh, rows(c), :]
        kdf = (kf_ref[0, hh, rows(c), :].astype(F32) * jnp.exp(bf[c_ - 1:c_] - bf)).astype(BF16)
        kdb = (kb_ref[0, hh, rows(c), :].astype(F32) * jnp.exp(bb[0:1] - bb)).astype(BF16)
        u = _dot_tn(v_ref[0, hh, rows(c), :], jnp.concatenate([kdf, kdb], axis=1))
        sf_ref[hh, c] = u[:, :HG_DK]
        sb_ref[hh, c] = u[:, HG_DK:]
        return carry

    lax.fori_loop(0, nc, chunk_kv, 0, unroll=min(nc, CHUNK_UNROLL))

    if has_s0:
        s_f, s_b = s0_ref[0, 0, hh].T, s0_ref[0, 1, hh].T
    else:
        s_f = s_b = jnp.zeros((HG_DV, HG_DK), F32)

    def scan(j, carry):
        s_f, s_b = carry
        cb = nc - 1 - j
        u_f, u_b = sf_ref[hh, j], sb_ref[hh, cb]
        sf_ref[hh, j] = s_f
        sb_ref[hh, cb] = s_b
        last = pl.multiple_of(j * c_, c_) + (c_ - 1)
        first = pl.multiple_of(cb * c_, c_)
        return (jnp.exp(bf_ref[hh, pl.ds(last, 1), :]) * s_f + u_f,
                jnp.exp(bb_ref[hh, pl.ds(first, 1), :]) * s_b + u_b)

    s_f, s_b = lax.fori_loop(0, nc, scan, (s_f, s_b), unroll=nc <= 4)
    if emit_state:
        st_ref[0, 0, hh] = s_f.T
        st_ref[0, 1, hh] = s_b.T

    def chunk_out(c, carry):
        q = q_ref[0, hh, rows(c), :].astype(F32)
        kf = kf_ref[0, hh, rows(c), :].astype(F32)
        kb = kb_ref[0, hh, rows(c), :].astype(F32)
        v = v_ref[0, hh, rows(c), :]
        bf = bf_ref[hh, rows(c), :]
        bb = bb_ref[hh, rows(c), :]
        ref_f = bf[mid:mid + 1]
        ref_b = bb[c_ - 1 - mid:c_ - mid]
        qq = jnp.concatenate([(q * jnp.exp(bf - ref_f)).astype(BF16), (q * jnp.exp(bb - ref_b)).astype(BF16)], axis=0)
        kk = jnp.concatenate([(kf * jnp.exp(ref_f - bf)).astype(BF16), (kb * jnp.exp(ref_b - bb)).astype(BF16)], axis=0)
        pr = _dot_nt(qq, kk)
        sc = jnp.where(mask_f, pr[:c_], 0.0) + jnp.where(mask_b, pr[c_:], 0.0)
        qcat = jnp.concatenate([(q * jnp.exp(bf)).astype(BF16), (q * jnp.exp(bb)).astype(BF16)], axis=1)
        scat = jnp.concatenate([sf_ref[hh, c], sb_ref[hh, c]], axis=1).astype(BF16)
        o = _dot(sc.astype(BF16), jnp.concatenate([v, v], axis=0)) + _dot_nt(qcat, scat)
        o_ref[0, hh, rows(c), :] = (_rms(o) * ng_ref[...]).astype(BF16)
        return carry

    lax.fori_loop(0, nc, chunk_out, 0, unroll=min(nc, CHUNK_UNROLL))


def _gla(q, kf, kb, gf, gb, v, norm_g, s0, emit_state, hps=1):
    b, nh, t, _ = q.shape
    c_ = GLA_CHUNK
    nc = t // c_
    has_s0 = s0 is not None
    kern = functools.partial(_per_head, _gla_kernel, hps=hps, chunk=c_, nc=nc, has_s0=has_s0,
                             emit_state=emit_state)
    seq = pl.BlockSpec((1, hps, t, HG_DK), lambda bi, h: (bi, h, 0, 0))
    in_specs = [pl.BlockSpec((1, HG_DV), lambda bi, h: (0, 0))] + [seq] * 6
    args = [norm_g.reshape(1, HG_DV), q, kf, kb, gf, gb, v]
    st_spec = pl.BlockSpec((1, 2, hps, HG_DK, HG_DV), lambda bi, h: (bi, 0, h, 0, 0))
    if has_s0:
        in_specs.append(st_spec)
        args.append(s0)
    out_specs = [pl.BlockSpec((1, hps, t, HG_DV), lambda bi, h: (bi, h, 0, 0))]
    out_shape = [jax.ShapeDtypeStruct((b, nh, t, HG_DV), BF16)]
    if emit_state:
        out_specs.append(st_spec)
        out_shape.append(jax.ShapeDtypeStruct((b, 2, nh, HG_DK, HG_DV), F32))
    return pl.pallas_call(
        kern,
        grid=(b, nh // hps),
        in_specs=in_specs,
        out_specs=out_specs,
        out_shape=out_shape,
        scratch_shapes=[pltpu.VMEM((hps, nc, HG_DV, HG_DK), F32)] * 2 + [pltpu.VMEM((hps, t, HG_DK), F32)] * 2,
        compiler_params=_cparams("parallel", "parallel"),
        name="gla",
    )(*args)


def _out_kernel(o_ref, g_ref, x_ref, w_ref, mod_ref, gpost_ref, y_ref):
    d = x_ref.shape[-1]
    o = jnp.concatenate([o_ref[0, h] for h in range(o_ref.shape[1])], axis=-1)
    y = _rms(_dot(o * g_ref[0], w_ref[...])) * gpost_ref[...]
    y_ref[0] = x_ref[0] + mod_ref[0, :, 2 * d:3 * d] * y


def _out_call(o, g, x, w, mod, gpost, tm):
    b, t, d = x.shape
    nh, dv = o.shape[1], o.shape[3]
    wd = nh * dv
    bm = mod.shape[0]
    mod_idx = (lambda bi, i: (bi, 0, 0)) if bm > 1 else (lambda bi, i: (0, 0, 0))
    return pl.pallas_call(
        _out_kernel,
        grid=(b, t // tm),
        in_specs=[
            pl.BlockSpec((1, nh, tm, dv), lambda bi, i: (bi, 0, i, 0)),
            pl.BlockSpec((1, tm, wd), lambda bi, i: (bi, i, 0)),
            pl.BlockSpec((1, tm, d), lambda bi, i: (bi, i, 0)),
            pl.BlockSpec((wd, d), lambda bi, i: (0, 0)),
            pl.BlockSpec((1, 1, mod.shape[2]), mod_idx),
            pl.BlockSpec((1, d), lambda bi, i: (0, 0)),
        ],
        out_specs=pl.BlockSpec((1, tm, d), lambda bi, i: (bi, i, 0)),
        out_shape=jax.ShapeDtypeStruct((b, t, d), F32),
        compiler_params=_cparams("parallel", "parallel"),
        name="out_proj",
    )(o, g, x, w, mod, gpost)


def _rope_tables(t):
    rows = t // GRID_W
    row = jnp.repeat(jnp.arange(rows), GRID_W)
    col = jnp.broadcast_to(jnp.arange(GRID_W), (rows, GRID_W)).reshape(-1)
    n_pair = DA_HEAD_DIM // 4
    inv = ROPE_BASE ** (-jnp.arange(n_pair, dtype=F32) / n_pair)
    ang = jnp.concatenate([row[:, None] * inv, col[:, None] * inv], axis=-1)
    cos = jnp.repeat(jnp.cos(ang), 2, axis=-1)
    sin = jnp.repeat(jnp.sin(ang), 2, axis=-1)
    odd = (jnp.arange(DA_HEAD_DIM) % 2) == 1
    sin_prev = jnp.where(odd, sin, 0.0)
    sin_next = jnp.where(odd, 0.0, -sin)
    tile = lambda a: jnp.tile(a, (1, LANES // DA_HEAD_DIM))
    return tile(cos), tile(sin_prev), tile(sin_next)


def kernel(x_prompt, x_sample, cache_k, cache_v, state_ret, state_hgrn, c, c_ctx,
           w_mod, b_mod, g_pre, g_post,
           da_w_in, da_w_out, da_lambda, da_subln,
           ret_w_in, ret_w_out, ret_decay,
           hg_w_in, hg_w_out, hg_lb, hg_norm):
    depth, d, _ = w_mod.shape
    db = x_sample.shape[0]
    ts = x_sample.shape[1]
    tp = x_prompt.shape[1]
    n_mix = 3
    kinds = [l % n_mix for l in range(depth)]
    slots = [kinds[:l].count(kinds[l]) for l in range(depth)]

    cvec = jnp.zeros((16, d), F32).at[:db].set(c).at[db].set(c_ctx)
    mods = _modulation_all(cvec, w_mod, b_mod)

    rope = _rope_tables(ts)
    tm = min(256, tp)
    tms = min(512, ts)
    tab_spec = pl.BlockSpec((tms, LANES), lambda bi, i: (i, 0))

    n_diff = kinds.count(0)
    past = cache_k.shape[2]
    cache_k = cache_k.reshape(db, n_diff, past, -1)
    cache_v = cache_v.reshape(db, n_diff, past, -1)
    xp, xs = x_prompt, x_sample
    new_k = new_v = None
    new_r, new_h = [], []
    for l in range(depth):
        kind, j = kinds[l], slots[l]
        mod_p = mods[l, db:db + 1].reshape(1, 1, 3 * d)
        mod_s = mods[l, :db].reshape(db, 1, 3 * d)
        gpre = g_pre[l].reshape(1, d)
        gpost = g_post[l].reshape(1, d)
        if kind == 0:
            w_in = da_w_in[j].astype(BF16)
            w_out = da_w_out[j].astype(BF16)
            heads = [("heads", LANES, BF16)] * 3 + [("plain", d, BF16)]
            carried = [] if new_k is None else [new_k, new_v]
            cache_outs = [("slot", d, F32, n_diff, j, new_k), ("slot", d, F32, n_diff, j, new_v)]
            qp, kp, vp, gp, new_k, new_v = _proj_call(
                functools.partial(_da_proj_kernel, rope=False, emit_cache=True,
                                  cache_slot=0 if carried else j, n_carried=len(carried)),
                xp, mod_p, gpre, w_in, [], [], heads + cache_outs, "da_proj_prompt", tm)
            qs, ks, vs, gs = _proj_call(
                functools.partial(_da_proj_kernel, rope=True, emit_cache=False),
                xs, mod_s, gpre, w_in, list(rope), [tab_spec] * 3, heads, "da_proj_sample", tms)
            op = _attention(qp, kp, vp, da_lambda[j], da_subln[j], l, tq=min(256, tp), hpu=N_HEADS)
            os_ = _attention(qs, ks, vs, da_lambda[j], da_subln[j], l, tq=min(512, ts), cache=(cache_k, cache_v, j))
        elif kind == 1:
            w_in = ret_w_in[j].astype(BF16)
            w_out = ret_w_out[j].astype(BF16)
            outs = [("heads", RET_DK, BF16)] * 2 + [("heads", RET_DV, BF16), ("plain", N_HEADS * RET_DV, BF16)]
            qp, kp, vp, gp = _proj_call(_ret_proj_kernel, xp, mod_p, gpre, w_in, [], [], outs,
                                        "ret_proj_prompt", tm)
            qs, ks, vs, gs = _proj_call(_ret_proj_kernel, xs, mod_s, gpre, w_in, [], [], outs,
                                        "ret_proj_sample", tms)
            dec = jnp.broadcast_to(ret_decay[j].T[:, :, None], (N_HEADS, 2, RET_DV))
            op, st = _retention(qp, kp, vp, dec, None, True, hps=N_HEADS)
            (os_,) = _retention(qs, ks, vs, dec, state_ret[:, j], False)
            new_r.append(st)
        else:
            w_in = hg_w_in[j].astype(BF16)
            w_out = hg_w_out[j].astype(BF16)
            lb = hg_lb.reshape(2 * depth, -1)
            lb_spec = pl.BlockSpec(lb.shape, lambda bi, i: (0, 0))
            outs = ([("heads", HG_DK, BF16)] * 3 + [("heads", HG_DK, F32)] * 2
                    + [("heads", HG_DV, BF16), ("plain", d, BF16)])
            kern = functools.partial(_hg_proj_kernel, layer_idx=l, depth=depth)
            pp = _proj_call(kern, xp, mod_p, gpre, w_in, [lb], [lb_spec], outs, "hg_proj_prompt", tm)
            ps = _proj_call(kern, xs, mod_s, gpre, w_in, [lb], [lb_spec], outs, "hg_proj_sample", tms)
            gp, gs = pp[6], ps[6]
            op, st = _gla(*pp[:6], hg_norm[j], None, True, hps=N_HEADS)
            (os_,) = _gla(*ps[:6], hg_norm[j], state_hgrn[:, j], False)
            new_h.append(st)
        xp = _out_call(op, gp, xp, w_out, mod_p, gpost, tm)
        xs = _out_call(os_, gs, xs, w_out, mod_s, gpost, min(1024, ts))

    bp = x_prompt.shape[0]
    new_cache_k = new_k.reshape(bp, n_diff, tp, 2 * N_HEADS, DA_HEAD_DIM)
    new_cache_v = new_v.reshape(bp, n_diff, tp, N_HEADS, DA_V_DIM)
    stack = lambda parts: parts[0][:, None] if len(parts) == 1 else jnp.stack(parts, axis=1)
    new_state_ret = stack(new_r)
    new_state_hgrn = stack(new_h)
    return (xp, xs, new_cache_k, new_cache_v, new_state_ret, new_state_hgrn)
```
